```python
import math
import jax, jax.numpy as jnp
from jax import lax
import numpy as np

D_MODEL = 1024
BATCH = 4
SEQ = 8192
DEPTH = 2
DEC_BATCH = 128
DEC_SEQ = 1
PAST_LEN = 16384
PAGE_SIZE = 128

HEAD_DIM = 64
N_GROUPS = 4
GROUP_W = 256
MIX_W = N_GROUPS * GROUP_W
NSA_HEADS = 4
L_CMP = 32
L_SEL = 64
N_SEL = 16
WINDOW = 512
FORCE_SCORE = 1.0e4
HG_HEADS = 4
HG_DK = 64
HG_DV = 64
HG_CHUNK = 64
FOX_HEADS = 4
FOX_KV_HEADS = 2
MLA_HEADS = 4
MLA_D_CQ = 192
MLA_D_C = 128
MLA_D_NOPE = 64
MLA_D_ROPE = 32
MLA_D_V = 64
ROPE_BASE = 10000.0
T5_BUCKETS = 32
T5_MAX_EXACT = 16
T5_MAX_DIST = 128
D_FF = 2816
N_SUB = 3
DEEPNORM_ALPHA = (2 * DEPTH) ** 0.25
DEEPNORM_BETA = (8 * DEPTH) ** -0.25
LN_EPS = 1e-5
RMS_EPS = 1e-6
Q_BLOCK = 128
NEG_INF = -1e30
TINY = 1e-30
IN_SIZES = (NSA_HEADS * HEAD_DIM, 6 * HEAD_DIM, 3 * NSA_HEADS,
            HG_HEADS * HG_DK, HG_HEADS * HG_DK, HG_HEADS * HG_DV, HG_HEADS * HG_DV,
            FOX_HEADS * HEAD_DIM, FOX_KV_HEADS * HEAD_DIM, FOX_KV_HEADS * HEAD_DIM, FOX_HEADS,
            MLA_D_CQ, MLA_D_C, MLA_D_ROPE)
IN_TOTAL = sum(IN_SIZES)

kernel_name = "hybrid_nsa_hgrn2_fox_mla_decoder_step"


def rmsnorm(x, g):
    xf = x.astype(jnp.float32)
    return (xf * lax.rsqrt(jnp.mean(xf * xf, axis=-1, keepdims=True) + RMS_EPS)).astype(x.dtype) * g


def layernorm(x, g, b):
    xf = x.astype(jnp.float32)
    mu = jnp.mean(xf, axis=-1, keepdims=True)
    var = jnp.mean(jnp.square(xf - mu), axis=-1, keepdims=True)
    return ((xf - mu) * lax.rsqrt(var + LN_EPS)).astype(x.dtype) * g + b


def masked_softmax(logits, mask):
    logits = jnp.where(mask, logits, NEG_INF)
    m = jnp.max(logits, axis=-1, keepdims=True)
    e = jnp.where(mask, jnp.exp(logits - m), 0.0)
    return e / jnp.maximum(jnp.sum(e, axis=-1, keepdims=True), TINY)


def rope(x, pos):
    half = x.shape[-1] // 2
    inv = ROPE_BASE ** (-jnp.arange(half, dtype=jnp.float32) / half)
    ang = pos.astype(jnp.float32)[:, None] * inv[None, :]
    ang = ang.reshape(ang.shape[:1] + (1,) * (x.ndim - 3) + ang.shape[1:])
    cos, sin = jnp.cos(ang).astype(x.dtype), jnp.sin(ang).astype(x.dtype)
    x1, x2 = x[..., :half], x[..., half:]
    return jnp.concatenate([x1 * cos - x2 * sin, x2 * cos + x1 * sin], axis=-1)


def t5_bias(table, dist):
    n = jnp.maximum(dist, 0)
    nf = jnp.maximum(n, 1).astype(jnp.float32)
    large = T5_MAX_EXACT + (jnp.log(nf / T5_MAX_EXACT) / math.log(T5_MAX_DIST / T5_MAX_EXACT)
                            * (T5_BUCKETS - T5_MAX_EXACT)).astype(jnp.int32)
    bucket = jnp.where(n < T5_MAX_EXACT, n, jnp.minimum(large, T5_BUCKETS - 1))
    return table[bucket].astype(jnp.float32)


def q_block_size(s):
    return Q_BLOCK if s % Q_BLOCK == 0 else s


def to_blocks(x, qb):
    n, s = x.shape[:2]
    return jnp.moveaxis(x.reshape((n, s // qb, qb) + x.shape[2:]), 1, 0)


def from_blocks(y):
    y = jnp.moveaxis(y, 0, 1)
    return y.reshape((y.shape[0], -1) + y.shape[3:])


def gather_paged(pool, page_table):
    g = pool[page_table]
    return g.reshape((page_table.shape[0], -1) + pool.shape[2:])


def swiglu(h, wg, wu, wd):
    return (jax.nn.silu(h @ wg) * (h @ wu)) @ wd


def nsa_cmp_sel(q, kc, vc, ks, vs, q_pos, cmp_w, t5_table):
    n, t_len, d = kc.shape
    scale = HEAD_DIM ** -0.5
    ncb = t_len // L_CMP
    w = jax.nn.softmax(cmp_w.astype(jnp.float32)).astype(kc.dtype)
    kcb = jnp.einsum('nbld,l->nbd', kc[:, :ncb * L_CMP].reshape(n, ncb, L_CMP, d), w)
    vcb = jnp.einsum('nbld,l->nbd', vc[:, :ncb * L_CMP].reshape(n, ncb, L_CMP, d), w)
    cmp_end = jnp.arange(ncb) * L_CMP + L_CMP - 1
    nsb = -(-t_len // L_SEL)
    pad = nsb * L_SEL - t_len
    ksb = jnp.pad(ks, ((0, 0), (0, pad), (0, 0))).reshape(n, nsb, L_SEL, d)
    vsb = jnp.pad(vs, ((0, 0), (0, pad), (0, 0))).reshape(n, nsb, L_SEL, d)
    k_top = min(N_SEL, nsb)
    sel_off = jnp.arange(L_SEL)
    blk = jnp.arange(nsb)

    def block(args):
        qb, pb = args
        nq = qb.shape[1]
        dist_c = pb[:, None] - cmp_end[None, :]
        s_c = jnp.einsum('nqhd,nbd->nhqb', qb, kcb).astype(jnp.float32) * scale \
            + jnp.moveaxis(t5_bias(t5_table, dist_c), -1, 0)[None]
        p_c = masked_softmax(s_c, (dist_c >= 0)[None, None])
        o_c = jnp.einsum('nhqb,nbd->nqhd', p_c.astype(vcb.dtype), vcb)
        imp = jnp.pad(p_c.sum(axis=1), ((0, 0), (0, 0), (0, 2 * nsb - ncb)))
        imp = imp.reshape(n, nq, nsb, 2).sum(-1)
        cur = pb // L_SEL
        valid = blk[None, :] <= cur[:, None]
        forced = (blk[None, :] == 0) | (blk[None, :] == cur[:, None]) | (blk[None, :] == cur[:, None] - 1)
        score = jnp.where(forced[None], FORCE_SCORE, jnp.where(valid[None], imp, -1.0))
        _, idx = lax.top_k(score, k_top)
        kg = jax.vmap(lambda kb, i: kb[i])(ksb, idx)
        vg = jax.vmap(lambda vb, i: vb[i])(vsb, idx)
        kpos = idx[..., None] * L_SEL + sel_off
        dist_s = pb[None, :, None, None] - kpos
        s_s = jnp.einsum('nqhd,nqkld->nhqkl', qb, kg).astype(jnp.float32) * scale \
            + jnp.moveaxis(t5_bias(t5_table, dist_s), -1, 1)
        p_s = masked_softmax(s_s.reshape(n, NSA_HEADS, nq, -1),
                             (dist_s >= 0).reshape(n, 1, nq, -1)).reshape(s_s.shape)
        o_s = jnp.einsum('nhqkl,nqkld->nqhd', p_s.astype(vg.dtype), vg)
        return o_c, o_s

    qb_ = q_block_size(q.shape[1])
    o_c, o_s = lax.map(block, (to_blocks(q, qb_), q_pos.reshape(-1, qb_)))
    return from_blocks(o_c), from_blocks(o_s)


def nsa_window(q, kw, vw, q_pos, k_pos, t5_table):
    n, nb, qb = q.shape[:3]
    dist = q_pos[:, :, None] - k_pos[:, None, :]
    mask = (dist >= 0) & (dist <= WINDOW) & (k_pos[:, None, :] >= 0)
    s = jnp.einsum('nbqhd,nbkd->nhbqk', q, kw).astype(jnp.float32) * HEAD_DIM ** -0.5 \
        + jnp.moveaxis(t5_bias(t5_table, dist), -1, 0)[None]
    p = masked_softmax(s, mask[None, None])
    o = jnp.einsum('nhbqk,nbkd->nbqhd', p.astype(vw.dtype), vw)
    return o.reshape(n, nb * qb, NSA_HEADS, HEAD_DIM)


def hgrn2_scan(q, logf, k, v, s0):
    n, s = q.shape[:2]
    c = HG_CHUNK if s % HG_CHUNK == 0 else s
    tri = jnp.tril(jnp.ones((c, c), dtype=bool))

    def chunk_fn(state, args):
        qc, lfc, kc, vc = args
        b = jnp.cumsum(lfc, axis=1)
        diff = b[:, :, None] - b[:, None, :]
        dec = jnp.where(tri[None, :, :, None, None], jnp.exp(jnp.minimum(diff, 0.0)), 0.0)
        a = jnp.einsum('nthk,nshk,ntshk->ntsh', qc, kc, dec)
        o = jnp.einsum('ntsh,nshv->nthv', a, vc) + jnp.einsum('nthk,nhkv->nthv', qc * jnp.exp(b), state)
        b_last = b[:, -1]
        new_state = jnp.exp(b_last)[..., None] * state \
            + jnp.einsum('nshk,nshv->nhkv', kc * jnp.exp(b_last[:, None] - b), vc)
        return new_state, o

    xs = tuple(to_blocks(t, c) for t in (q, logf, k, v))
    s_fin, o = lax.scan(chunk_fn, s0, xs)
    return from_blocks(o), s_fin


def fox_attend(q, k, v, c_q, c_k, q_pos):
    n, t_len = k.shape[:2]
    g = FOX_HEADS // FOX_KV_HEADS
    k_pos = jnp.arange(t_len)
    ck_t = c_k.reshape(n, t_len, FOX_KV_HEADS, g).transpose(0, 2, 3, 1)[:, :, :, None, :]

    def block(args):
        qb, cqb, pb = args
        nq = qb.shape[1]
        qg = qb.reshape(n, nq, FOX_KV_HEADS, g, HEAD_DIM)
        cq_t = cqb.reshape(n, nq, FOX_KV_HEADS, g).transpose(0, 2, 3, 1)[..., None]
        s = jnp.einsum('nqkgd,nskd->nkgqs', qg, k).astype(jnp.float32) * HEAD_DIM ** -0.5 + cq_t - ck_t
        p = masked_softmax(s, (pb[:, None] >= k_pos[None, :]))
        o = jnp.einsum('nkgqs,nskd->nqkgd', p.astype(v.dtype), v)
        return o.reshape(n, nq, FOX_HEADS, HEAD_DIM)

    qb_ = q_block_size(q.shape[1])
    o = lax.map(block, (to_blocks(q, qb_), to_blocks(c_q, qb_), q_pos.reshape(-1, qb_)))
    return from_blocks(o)


def mla_attend(q_lat, q_rope, ckv, krope, q_pos):
    k_pos = jnp.arange(ckv.shape[1])
    scale = (MLA_D_NOPE + MLA_D_ROPE) ** -0.5

    def block(args):
        ql, qr, pb = args
        s = (jnp.einsum('nqhc,nsc->nhqs', ql, ckv) + jnp.einsum('nqhr,nsr->nhqs', qr, krope)).astype(jnp.float32) * scale
        p = masked_softmax(s, (pb[:, None] >= k_pos[None, :]))
        return jnp.einsum('nhqs,nsc->nqhc', p.astype(ckv.dtype), ckv)

    qb_ = q_block_size(q_lat.shape[1])
    o = lax.map(block, (to_blocks(q_lat, qb_), to_blocks(q_rope, qb_), q_pos.reshape(-1, qb_)))
    return from_blocks(o)


def token_mixer(h, past, w_in_l, w_out_l, mix_g_l, cmp_w_l, t5_table, lb_l, fox_bf_l,
                mla_qg_l, mla_kvg_l, mla_wuq_l, mla_wuk_l, mla_wuv_l):
    n, s, _ = h.shape
    p_len = 0 if past is None else past[0].shape[1]
    q_pos = p_len + jnp.arange(s)
    offs = np.cumsum(IN_SIZES)[:-1].tolist()
    (nsa_q, nsa_kv, nsa_g, hg_q, hg_f, hg_i, hg_g, fox_q, fox_k, fox_v, fox_f,
     mla_cq, mla_ckv, mla_kr) = jnp.split(h @ w_in_l, offs, axis=-1)

    q_n = nsa_q.reshape(n, s, NSA_HEADS, HEAD_DIM)
    nsa_rows = nsa_kv[..., :4 * HEAD_DIM].reshape(n, s, 4, HEAD_DIM)
    win_rows = nsa_kv[..., 4 * HEAD_DIM:].reshape(n, s, 2, HEAD_DIM)
    nsa_full = nsa_rows if past is None else jnp.concatenate([past[0], nsa_rows], axis=1)
    o_c, o_s = nsa_cmp_sel(q_n, nsa_full[:, :, 0], nsa_full[:, :, 1], nsa_full[:, :, 2], nsa_full[:, :, 3],
                           q_pos, cmp_w_l, t5_table)
    if past is None:
        qb = q_block_size(s)
        nb = s // qb
        idx = jnp.arange(nb)[:, None] * qb + jnp.arange(WINDOW + qb)[None, :]
        win_pad = jnp.pad(win_rows, ((0, 0), (WINDOW, 0), (0, 0), (0, 0)))
        kv_blk = win_pad[:, idx]
        k_pos_w = idx - WINDOW
        q_blk = q_n.reshape(n, nb, qb, NSA_HEADS, HEAD_DIM)
        qpos_blk = q_pos.reshape(nb, qb)
        new_win = win_rows[:, s - min(WINDOW, s):]
    else:
        wb = past[1].shape[1]
        win_all = jnp.concatenate([past[1], win_rows], axis=1)
        kv_blk = win_all[:, None]
        k_pos_w = (p_len - wb + jnp.arange(wb + s))[None]
        q_blk = q_n[:, None]
        qpos_blk = q_pos[None]
        new_win = win_all[:, s:]
    o_w = nsa_window(q_blk, kv_blk[..., 0, :], kv_blk[..., 1, :], qpos_blk, k_pos_w, t5_table)
    gates = jax.nn.sigmoid(nsa_g.reshape(n, s, NSA_HEADS, 3))
    o_nsa = gates[..., 0:1] * o_c + gates[..., 1:2] * o_s + gates[..., 2:3] * o_w

    lb = lb_l.reshape(HG_HEADS, HG_DK).astype(jnp.float32)
    z = hg_f.reshape(n, s, HG_HEADS, HG_DK).astype(jnp.float32)
    f_h = lb + (1.0 - lb) * jax.nn.sigmoid(z)
    logf_h = jnp.log(f_h)
    k_h = (1.0 - lb) * jax.nn.sigmoid(-z)
    s0 = jnp.zeros((n, HG_HEADS, HG_DK, HG_DV), jnp.float32) if past is None else past[5].astype(jnp.float32)
    o_h, hg_state = hgrn2_scan(hg_q.reshape(n, s, HG_HEADS, HG_DK).astype(jnp.float32), logf_h, k_h,
                               hg_i.reshape(n, s, HG_HEADS, HG_DV).astype(jnp.float32), s0)

    q_f = fox_q.reshape(n, s, FOX_HEADS, HEAD_DIM)
    fox_rows = jnp.stack([fox_k.reshape(n, s, FOX_KV_HEADS, HEAD_DIM),
                          fox_v.reshape(n, s, FOX_KV_HEADS, HEAD_DIM)], axis=2)
    logf_new = jax.nn.log_sigmoid(fox_f.astype(jnp.float32) + fox_bf_l.astype(jnp.float32))
    if past is None:
        fox_full, logf_all = fox_rows, logf_new
    else:
        fox_full = jnp.concatenate([past[2], fox_rows], axis=1)
        logf_all = jnp.concatenate([past[3].astype(jnp.float32), logf_new], axis=1)
    cum = jnp.cumsum(logf_all, axis=1)
    o_f = fox_attend(q_f, fox_full[:, :, 0], fox_full[:, :, 1], cum[:, -s:], cum, q_pos)

    qm = (rmsnorm(mla_cq, mla_qg_l) @ mla_wuq_l).reshape(n, s, MLA_HEADS, MLA_D_NOPE + MLA_D_ROPE)
    q_nope, q_rope = qm[..., :MLA_D_NOPE], rope(qm[..., MLA_D_NOPE:], q_pos)
    mla_rows = jnp.concatenate([rmsnorm(mla_ckv, mla_kvg_l), rope(mla_kr, q_pos)], axis=-1)
    mla_full = mla_rows if past is None else jnp.concatenate([past[4], mla_rows], axis=1)
    q_lat = jnp.einsum('nshd,chd->nshc', q_nope, mla_wuk_l.reshape(MLA_D_C, MLA_HEADS, MLA_D_NOPE))
    o_lat = mla_attend(q_lat, q_rope, mla_full[..., :MLA_D_C], mla_full[..., MLA_D_C:], q_pos)
    o_m = jnp.einsum('nshc,chv->nshv', o_lat, mla_wuv_l.reshape(MLA_D_C, MLA_HEADS, MLA_D_V))

    groups = jnp.stack([o_nsa.reshape(n, s, GROUP_W), o_h.reshape(n, s, GROUP_W).astype(h.dtype),
                        o_f.reshape(n, s, GROUP_W), o_m.reshape(n, s, GROUP_W)], axis=2)
    groups = rmsnorm(groups, mix_g_l.reshape(N_GROUPS, GROUP_W))
    merged = jnp.concatenate([groups[:, :, 0], groups[:, :, 1] * jax.nn.silu(hg_g),
                              groups[:, :, 2], groups[:, :, 3]], axis=-1)
    return merged @ w_out_l, (nsa_rows, new_win, fox_rows, logf_new, mla_rows, hg_state)


def layer(x, c, past, mix_w, ada_w_l, ada_b_l, ln_g_l, ln_b_l, wg_l, wu_l, wd_l):
    n, _, d = x.shape
    ada = (jax.nn.silu(c) @ ada_w_l + ada_b_l).reshape(n, N_SUB, 3, d)

    def modulate(x, j):
        return x * (1.0 + ada[:, j, 1, None]) + ada[:, j, 0, None]

    def post(x, y, j, w):
        return layernorm(DEEPNORM_ALPHA * x + w * (1.0 + ada[:, j, 2, None]) * y, ln_g_l[j], ln_b_l[j])

    x = post(x, swiglu(modulate(x, 0), wg_l[0], wu_l[0], wd_l[0]), 0, 0.5)
    y, st = token_mixer(modulate(x, 1), past, *mix_w)
    x = post(x, y, 1, 1.0)
    x = post(x, swiglu(modulate(x, 2), wg_l[1], wu_l[1], wd_l[1]), 2, 0.5)
    return x, st


def setup_inputs(seed: int = 0) -> dict:
    key = jax.random.key(seed)
    ks = list(jax.random.split(key, 40))

    def nrm(shape, scale):
        return scale * jax.random.normal(ks.pop(), shape, jnp.float32)

    n_pages = PAST_LEN // PAGE_SIZE
    n_used = DEC_BATCH * n_pages
    n_phys = n_used + max(1, n_used // 4)
    page_table = jax.random.permutation(ks.pop(), n_phys)[:n_used].reshape(DEC_BATCH, n_pages).astype(jnp.int32)
    win_buf = min(WINDOW, PAST_LEN)
    return {
        'x_prompt': nrm((BATCH, SEQ, D_MODEL), 1.0),
        'x_sample': nrm((DEC_BATCH, DEC_SEQ, D_MODEL), 1.0),
        'c_prompt': nrm((BATCH, D_MODEL), 1.0),
        'c_sample': nrm((DEC_BATCH, D_MODEL), 1.0),
        'page_table': page_table,
        'cache_nsa_kv': nrm((DEPTH, n_phys, PAGE_SIZE, 4, HEAD_DIM), 1.0),
        'cache_nsa_win': nrm((DEPTH, DEC_BATCH, win_buf, 2, HEAD_DIM), 1.0),
        'cache_fox_kv': nrm((DEPTH, n_phys, PAGE_SIZE, 2, FOX_KV_HEADS, HEAD_DIM), 1.0),
        'cache_fox_logf': jax.nn.log_sigmoid(3.0 + nrm((DEPTH, n_phys, PAGE_SIZE, FOX_HEADS), 1.0)),
        'cache_mla': nrm((DEPTH, n_phys, PAGE_SIZE, MLA_D_C + MLA_D_ROPE), 1.0),
        'state_hgrn': nrm((DEPTH, DEC_BATCH, HG_HEADS, HG_DK, HG_DV), 0.5),
        'w_in': nrm((DEPTH, D_MODEL, IN_TOTAL), D_MODEL ** -0.5),
        'w_out': nrm((DEPTH, MIX_W, D_MODEL), MIX_W ** -0.5 * DEEPNORM_BETA),
        'mix_norm_g': 1.0 + nrm((DEPTH, MIX_W), 0.02),
        'nsa_cmp_w': nrm((DEPTH, L_CMP), 0.3),
        't5_table': nrm((T5_BUCKETS, NSA_HEADS), 0.5),
        'hgrn_lb_logits': nrm((DEPTH, HG_HEADS * HG_DK), 1.0),
        'fox_b_f': 2.0 + nrm((DEPTH, FOX_HEADS), 0.5),
        'mla_q_norm_g': 1.0 + nrm((DEPTH, MLA_D_CQ), 0.02),
        'mla_kv_norm_g': 1.0 + nrm((DEPTH, MLA_D_C), 0.02),
        'mla_w_uq': nrm((DEPTH, MLA_D_CQ, MLA_HEADS * (MLA_D_NOPE + MLA_D_ROPE)), MLA_D_CQ ** -0.5),
        'mla_w_uk': nrm((DEPTH, MLA_D_C, MLA_HEADS * MLA_D_NOPE), MLA_D_C ** -0.5),
        'mla_w_uv': nrm((DEPTH, MLA_D_C, MLA_HEADS * MLA_D_V), MLA_D_C ** -0.5),
        'ffn_w_gate': nrm((DEPTH, 2, D_MODEL, D_FF), D_MODEL ** -0.5),
        'ffn_w_up': nrm((DEPTH, 2, D_MODEL, D_FF), D_MODEL ** -0.5),
        'ffn_w_down': nrm((DEPTH, 2, D_FF, D_MODEL), D_FF ** -0.5 * DEEPNORM_BETA),
        'ada_w': nrm((DEPTH, D_MODEL, N_SUB * 3 * D_MODEL), 0.1 * D_MODEL ** -0.5),
        'ada_b': nrm((DEPTH, N_SUB * 3 * D_MODEL), 0.01),
        'ln_g': 1.0 + nrm((DEPTH, N_SUB, D_MODEL), 0.02),
        'ln_b': nrm((DEPTH, N_SUB, D_MODEL), 0.02),
    }


def reference(x_prompt, x_sample, c_prompt, c_sample, page_table, cache_nsa_kv, cache_nsa_win,
              cache_fox_kv, cache_fox_logf, cache_mla, state_hgrn, w_in, w_out, mix_norm_g, nsa_cmp_w,
              t5_table, hgrn_lb_logits, fox_b_f, mla_q_norm_g, mla_kv_norm_g, mla_w_uq, mla_w_uk, mla_w_uv,
              ffn_w_gate, ffn_w_up, ffn_w_down, ada_w, ada_b, ln_g, ln_b):
    lb_p = jax.nn.softmax(hgrn_lb_logits.astype(jnp.float32), axis=0)
    lb_all = jnp.cumsum(lb_p, axis=0) - lb_p
    xp, xs = x_prompt, x_sample
    st_p, st_s = [], []
    for l in range(DEPTH):
        mix_w = (w_in[l], w_out[l], mix_norm_g[l], nsa_cmp_w[l], t5_table, lb_all[l], fox_b_f[l],
                 mla_q_norm_g[l], mla_kv_norm_g[l], mla_w_uq[l], mla_w_uk[l], mla_w_uv[l])
        ffn_w = (ada_w[l], ada_b[l], ln_g[l], ln_b[l], ffn_w_gate[l], ffn_w_up[l], ffn_w_down[l])
        past = (gather_paged(cache_nsa_kv[l], page_table), cache_nsa_win[l],
                gather_paged(cache_fox_kv[l], page_table), gather_paged(cache_fox_logf[l], page_table),
                gather_paged(cache_mla[l], page_table), state_hgrn[l])
        xp, sp = layer(xp, c_prompt, None, mix_w, *ffn_w)
        xs, ss = layer(xs, c_sample, past, mix_w, *ffn_w)
        st_p.append(sp)
        st_s.append(ss)
    nsa_kv_p = jnp.stack([t[0] for t in st_p])
    nsa_win_p = jnp.stack([t[1] for t in st_p])
    fox_kv_p = jnp.stack([t[2] for t in st_p])
    fox_logf_p = jnp.stack([t[3] for t in st_p])
    mla_p = jnp.stack([t[4] for t in st_p])
    hgrn_p = jnp.stack([t[5] for t in st_p])
    nsa_kv_s = jnp.stack([t[0] for t in st_s])
    nsa_win_s = jnp.stack([t[1] for t in st_s])
    fox_kv_s = jnp.stack([t[2] for t in st_s])
    fox_logf_s = jnp.stack([t[3] for t in st_s])
    mla_s = jnp.stack([t[4] for t in st_s])
    hgrn_s = jnp.stack([t[5] for t in st_s])
    return (xp, xs, nsa_kv_p, nsa_win_p, fox_kv_p, fox_logf_p, mla_p, hgrn_p,
            nsa_kv_s, nsa_win_s, fox_kv_s, fox_logf_s, mla_s, hgrn_s)
```

```python
import functools
import math

import jax
import jax.numpy as jnp
import numpy as np
from jax import lax
from jax.experimental import pallas as pl
from jax.experimental.pallas import tpu as pltpu

F32 = jnp.float32
BF16 = jnp.bfloat16

HEAD_DIM = 64
N_GROUPS = 4
GROUP_W = 256
MIX_W = N_GROUPS * GROUP_W
NSA_HEADS = 4
L_CMP = 32
L_SEL = 64
N_SEL = 16
WINDOW = 512
FORCE_SCORE = 1.0e4
HG_HEADS = 4
HG_DK = 64
HG_DV = 64
FOX_HEADS = 4
FOX_KV_HEADS = 2
MLA_HEADS = 4
MLA_D_CQ = 192
MLA_D_C = 128
MLA_D_NOPE = 64
MLA_D_ROPE = 32
MLA_D_V = 64
ROPE_BASE = 10000.0
T5_BUCKETS = 32
T5_MAX_EXACT = 16
T5_MAX_DIST = 128
N_SUB = 3
LN_EPS = 1e-5
RMS_EPS = 1e-6
NEG_INF = -1e30
TINY = 1e-30
PAGE_SIZE = 128

LANES = 128
SUBLANES = 8
VMEM_LIMIT = 56 * 1024 * 1024

IN_SIZES = (NSA_HEADS * HEAD_DIM, 6 * HEAD_DIM, 3 * NSA_HEADS,
            HG_HEADS * HG_DK, HG_HEADS * HG_DK, HG_HEADS * HG_DV, HG_HEADS * HG_DV,
            FOX_HEADS * HEAD_DIM, FOX_KV_HEADS * HEAD_DIM, FOX_KV_HEADS * HEAD_DIM, FOX_HEADS,
            MLA_D_CQ, MLA_D_C, MLA_D_ROPE)
IN_OFFS = tuple(int(v) for v in np.cumsum((0,) + IN_SIZES))


def _t5_thresholds():
    d = np.arange(0, 4 * T5_MAX_DIST)
    nf = np.maximum(d, 1).astype(np.float64)
    large = T5_MAX_EXACT + (np.log(nf / T5_MAX_EXACT) / math.log(T5_MAX_DIST / T5_MAX_EXACT)
                            * (T5_BUCKETS - T5_MAX_EXACT)).astype(np.int64)
    bucket = np.where(d < T5_MAX_EXACT, d, np.minimum(large, T5_BUCKETS - 1))
    return tuple(int(np.argmax(bucket >= b)) for b in range(1, T5_BUCKETS))


T5_THR = _t5_thresholds()


def _cparams(*sem):
    return pltpu.CompilerParams(dimension_semantics=sem, vmem_limit_bytes=VMEM_LIMIT)


def _dot(a, b):
    return jnp.dot(a, b, preferred_element_type=F32)


def _dot_nt(a, b):
    return lax.dot_general(a, b, (((1,), (1,)), ((), ())), preferred_element_type=F32)


def _dot_tn(a, b):
    return lax.dot_general(a, b, (((0,), (0,)), ((), ())), preferred_element_type=F32)


def _dot_hi(a, b):
    return jnp.dot(a, b, preferred_element_type=F32, precision=lax.Precision.HIGHEST)


def _sigmoid(x):
    return 1.0 / (1.0 + jnp.exp(-x))


def _silu(x):
    return x * _sigmoid(x)


def _t5_bias(dist, t5_ref, heads=NSA_HEADS):
    outs = [jnp.full(dist.shape, t5_ref[0, h], F32) for h in range(heads)]
    for b in range(1, T5_BUCKETS):
        ge = dist >= T5_THR[b - 1]
        outs = [jnp.where(ge, t5_ref[b, h], o) for h, o in enumerate(outs)]
    return outs


def _ada_kernel(c_ref, w_ref, b_ref, o_ref):
    c = _silu(c_ref[...]).astype(BF16)
    o_ref[...] = _dot(c, w_ref[...].astype(BF16)) + b_ref[...]


def ada_call(c, w, b):
    m, d = c.shape
    n = w.shape[1]
    tn = 1152 if n % 1152 == 0 else n
    return pl.pallas_call(
        _ada_kernel,
        grid=(n // tn,),
        in_specs=[pl.BlockSpec((m, d), lambda j: (0, 0)),
                  pl.BlockSpec((d, tn), lambda j: (0, j)),
                  pl.BlockSpec((1, tn), lambda j: (0, j))],
        out_specs=pl.BlockSpec((m, tn), lambda j: (0, j)),
        out_shape=jax.ShapeDtypeStruct((m, n), F32),
        compiler_params=_cparams("arbitrary"),
        name="ada",
    )(c, w, b.reshape(1, n))


def _post_ln(x, y, g, b, alpha):
    z = alpha * x + y
    mu = jnp.mean(z, axis=-1, keepdims=True)
    zc = z - mu
    var = jnp.mean(zc * zc, axis=-1, keepdims=True)
    return zc * lax.rsqrt(var + LN_EPS) * g + b


FF_CHUNK = 256


def _ffn_kernel(x_ref, sh_ref, sc_ref, gt_ref, wg_ref, wu_ref, wd_ref, lg_ref, lb_ref, o_ref, *, alpha):
    x = x_ref[0]
    h = (x * (1.0 + sc_ref[0]) + sh_ref[0]).astype(BF16)
    d_ff = wg_ref.shape[1]
    acc = jnp.zeros(x.shape, F32)
    for c0 in range(0, d_ff, FF_CHUNK):
        g = _dot(h, wg_ref[:, c0:c0 + FF_CHUNK])
        u = _dot(h, wu_ref[:, c0:c0 + FF_CHUNK])
        a = (_silu(g) * u).astype(BF16)
        acc = acc + _dot(a, wd_ref[c0:c0 + FF_CHUNK, :])
    y = 0.5 * (1.0 + gt_ref[0]) * acc
    o_ref[0] = _post_ln(x, y, lg_ref[...], lb_ref[...], alpha)


def ffn_call(x, sh, sc, gt, wg, wu, wd, lg, lb, alpha, tm):
    g_, r_, d = x.shape
    ra = sh.shape[1]
    d_ff = wg.shape[1]
    assert d_ff % FF_CHUNK == 0 and r_ % tm == 0
    if ra == 1:
        a_spec = pl.BlockSpec((1, 1, d), lambda g, i: (g, 0, 0))
    else:
        a_spec = pl.BlockSpec((1, tm, d), lambda g, i: (g, i, 0))
    const = dict(pipeline_mode=pl.Buffered(1))
    return pl.pallas_call(
        functools.partial(_ffn_kernel, alpha=alpha),
        grid=(g_, r_ // tm),
        in_specs=[pl.BlockSpec((1, tm, d), lambda g, i: (g, i, 0)), a_spec, a_spec, a_spec,
                  pl.BlockSpec((d, d_ff), lambda g, i: (0, 0), **const),
                  pl.BlockSpec((d, d_ff), lambda g, i: (0, 0), **const),
                  pl.BlockSpec((d_ff, d), lambda g, i: (0, 0), **const),
                  pl.BlockSpec((1, d), lambda g, i: (0, 0)),
                  pl.BlockSpec((1, d), lambda g, i: (0, 0))],
        out_specs=pl.BlockSpec((1, tm, d), lambda g, i: (g, i, 0)),
        out_shape=jax.ShapeDtypeStruct(x.shape, F32),
        compiler_params=_cparams("parallel", "parallel"),
        name="ffn",
    )(x, sh, sc, gt, wg, wu, wd, lg.reshape(1, d), lb.reshape(1, d))


PROJ = {
    "fox_qp": (0, 512),
    "nsa_q": (512, 256),
    "nsa_rows": (768, 256),
    "hg_q": (1024, 256),
    "hg_f": (1280, 256),
    "hg_i": (1536, 256),
    "hg_g": (1792, 256),
    "fox_kp": (2048, 256),
    "fox_kv": (2304, 256),
    "mla_cq": (2560, 256),
    "win_rows": (2816, 128),
    "ksw": (2944, 128),
    "vsw": (3072, 128),
    "mla_ckv": (3200, 128),
    "misc": (3328, 128),
}
PROJ_W = 3456
MISC_KR, MISC_G, MISC_F = 0, 32, 44


def _proj_src_columns():
    o = dict(zip(("nsa_q", "nsa_kv", "nsa_g", "hg_q", "hg_f", "hg_i", "hg_g", "fox_q", "fox_k", "fox_v",
                  "fox_f", "mla_cq", "mla_ckv", "mla_kr"), IN_OFFS[:-1]))
    src = -np.ones((PROJ_W,), np.int64)

    def put(name, at, cols):
        base = PROJ[name][0] + at
        src[base:base + len(cols)] = cols

    hd = HEAD_DIM
    for h in range(FOX_HEADS):
        put("fox_qp", 128 * h, o["fox_q"] + hd * h + np.arange(hd))
    put("nsa_q", 0, o["nsa_q"] + np.arange(256))
    put("nsa_rows", 0, o["nsa_kv"] + np.arange(256))
    for nm in ("hg_q", "hg_f", "hg_i", "hg_g"):
        put(nm, 0, o[nm] + np.arange(256))
    for j in range(FOX_KV_HEADS):
        put("fox_kp", 128 * j, o["fox_k"] + hd * j + np.arange(hd))
    put("fox_kv", 0, o["fox_k"] + np.arange(256))
    put("mla_cq", 0, o["mla_cq"] + np.arange(MLA_D_CQ))
    put("win_rows", 0, o["nsa_kv"] + 4 * hd + np.arange(2 * hd))
    put("ksw", 0, o["nsa_kv"] + 2 * hd + np.arange(hd))
    put("ksw", hd, o["nsa_kv"] + 4 * hd + np.arange(hd))
    put("vsw", 0, o["nsa_kv"] + 3 * hd + np.arange(hd))
    put("vsw", hd, o["nsa_kv"] + 5 * hd + np.arange(hd))
    put("mla_ckv", 0, o["mla_ckv"] + np.arange(MLA_D_C))
    put("misc", MISC_KR, o["mla_kr"] + np.arange(MLA_D_ROPE))
    put("misc", MISC_G, o["nsa_g"] + np.arange(3 * NSA_HEADS))
    put("misc", MISC_F, o["fox_f"] + np.arange(FOX_HEADS))
    return src


PROJ_SRC = _proj_src_columns()


def pad_w_in(w_in_l):
    cols = jnp.asarray(np.maximum(PROJ_SRC, 0), jnp.int32)
    w = jnp.take(w_in_l, cols, axis=1)
    return jnp.where(jnp.asarray(PROJ_SRC >= 0)[None, :], w, 0.0).astype(BF16)


def pspec(name, tm):
    off, w = PROJ[name]
    return pl.BlockSpec((1, tm, w), lambda g, i, _b=off // w: (g, i, _b))


def _inproj_kernel(x_ref, sh_ref, sc_ref, w_ref, o_ref):
    h = (x_ref[0] * (1.0 + sc_ref[0]) + sh_ref[0]).astype(BF16)
    o_ref[0] = _dot(h, w_ref[...])


def inproj_call(x, sh, sc, w_pad, tm):
    g_, r_, d = x.shape
    ra = sh.shape[1]
    n = w_pad.shape[1]
    if ra == 1:
        a_spec = pl.BlockSpec((1, 1, d), lambda g, i: (g, 0, 0))
    else:
        a_spec = pl.BlockSpec((1, tm, d), lambda g, i: (g, i, 0))
    return pl.pallas_call(
        _inproj_kernel,
        grid=(g_, r_ // tm),
        in_specs=[pl.BlockSpec((1, tm, d), lambda g, i: (g, i, 0)), a_spec, a_spec,
                  pl.BlockSpec((d, n), lambda g, i: (0, 0), pipeline_mode=pl.Buffered(1))],
        out_specs=pl.BlockSpec((1, tm, n), lambda g, i: (g, i, 0)),
        out_shape=jax.ShapeDtypeStruct((g_, r_, n), F32),
        compiler_params=_cparams("parallel", "parallel"),
        name="inproj",
    )(x, sh, sc, w_pad)


MLA_DKP = 256
MLA_SCALE = (MLA_D_NOPE + MLA_D_ROPE) ** -0.5


def mla_weights(qg, kvg, wuq, wuk):
    qg_p = jnp.zeros((1, 256), F32).at[0, :MLA_D_CQ].set(qg)
    dq = MLA_D_NOPE + MLA_D_ROPE
    half = MLA_D_ROPE // 2
    cols = np.zeros((384,), np.int64)
    for h in range(MLA_HEADS):
        cols[64 * h:64 * h + 64] = dq * h + np.arange(64)
        cols[256 + half * h:256 + half * (h + 1)] = dq * h + MLA_D_NOPE + np.arange(half)
        cols[320 + half * h:320 + half * (h + 1)] = dq * h + MLA_D_NOPE + half + np.arange(half)
    wuq_p = jnp.zeros((256, 384), F32).at[:MLA_D_CQ].set(jnp.take(wuq, jnp.asarray(cols, jnp.int32), axis=1))
    wk = wuk.reshape(MLA_D_C, MLA_HEADS, MLA_D_NOPE)
    wcomb = jnp.zeros((384, MLA_HEADS * MLA_DKP), F32)
    perm = np.zeros((384, MLA_HEADS * MLA_DKP), np.float32)
    for h in range(MLA_HEADS):
        wcomb = wcomb.at[64 * h:64 * h + 64, MLA_DKP * h:MLA_DKP * h + MLA_D_C].set(wk[:, h, :].T)
        for i in range(half):
            perm[256 + half * h + i, MLA_DKP * h + MLA_D_C + i] = 1.0
            perm[320 + half * h + i, MLA_DKP * h + MLA_D_C + half + i] = 1.0
    wcomb = wcomb + jnp.asarray(perm)
    return qg_p, kvg.reshape(1, MLA_D_C), wuq_p.astype(BF16), wcomb.astype(BF16)


def rope_tables(pos):
    half = MLA_D_ROPE // 2
    inv = ROPE_BASE ** (-jnp.arange(half, dtype=F32) / half)
    ang = pos.astype(F32)[:, None] * inv[None, :]
    cos, sin = jnp.cos(ang), jnp.sin(ang)
    z = jnp.zeros((pos.shape[0], 128 - 2 * half), F32)
    zh = jnp.zeros_like(cos)
    cos4, sin4 = jnp.tile(cos, (1, MLA_HEADS)), jnp.tile(sin, (1, MLA_HEADS))
    q_c = jnp.concatenate([cos4, cos4], axis=1)
    q_s = jnp.concatenate([-sin4, sin4], axis=1)
    k_c = jnp.concatenate([cos, cos, z], axis=1)
    k_sa = jnp.concatenate([-sin, zh, z], axis=1)
    k_sb = jnp.concatenate([zh, sin, z], axis=1)
    return q_c, q_s, k_c, k_sa, k_sb


def _mla_prep_kernel(cq_ref, ckv_ref, misc_ref, qg_ref, kvg_ref, wuq_ref, wcomb_ref,
                     qc_ref, qs_ref, kc_ref, ksa_ref, ksb_ref, q_ref, rows_ref, kb_ref, vt_ref):
    cq = cq_ref[0]
    ms = jnp.sum(cq * cq, axis=-1, keepdims=True) * (1.0 / MLA_D_CQ)
    qn = ((cq * lax.rsqrt(ms + RMS_EPS)) * qg_ref[...]).astype(BF16)
    qm = _dot(qn, wuq_ref[...])
    t = qm[:, 256:384]
    r = t * qc_ref[...] + pltpu.roll(t, 64, axis=1) * qs_ref[...]
    qcat = jnp.concatenate([qm[:, :256], r], axis=1).astype(BF16)
    q_ref[0] = (_dot(qcat, wcomb_ref[...]) * MLA_SCALE).astype(BF16)
    ckv = ckv_ref[0]
    ms2 = jnp.mean(ckv * ckv, axis=-1, keepdims=True)
    ckvn = (ckv * lax.rsqrt(ms2 + RMS_EPS)) * kvg_ref[...]
    misc = misc_ref[0]
    kr = (misc * kc_ref[...] + pltpu.roll(misc, 128 - MLA_D_ROPE // 2, axis=1) * ksa_ref[...]
          + pltpu.roll(misc, MLA_D_ROPE // 2, axis=1) * ksb_ref[...])
    kfull = jnp.concatenate([ckvn, kr], axis=1)
    rows_ref[0] = kfull[:, :MLA_D_C + MLA_D_ROPE]
    if kb_ref is not None:
        kb_ref[0, 0] = kfull.astype(BF16)
        vt_ref[0, 0] = ckvn.T.astype(BF16)


def mla_prep_call(proj, wts, tabs, tm, with_kv):
    g_, r_, _ = proj.shape
    qg_p, kvg, wuq_p, wcomb = wts
    nt = r_ // tm
    if tabs[0].shape[0] == 1:
        t_spec = pl.BlockSpec((1, 128), lambda g, i: (0, 0))
    else:
        t_spec = pl.BlockSpec((tm, 128), lambda g, i: (i, 0))
    c2 = lambda g, i: (0, 0)
    out_shape = [jax.ShapeDtypeStruct((g_, r_, MLA_HEADS * MLA_DKP), BF16),
                 jax.ShapeDtypeStruct((g_, r_, MLA_D_C + MLA_D_ROPE), F32)]
    out_specs = [pl.BlockSpec((1, tm, MLA_HEADS * MLA_DKP), lambda g, i: (g, i, 0)),
                 pl.BlockSpec((1, tm, MLA_D_C + MLA_D_ROPE), lambda g, i: (g, i, 0))]
    if with_kv:
        out_shape += [jax.ShapeDtypeStruct((g_, nt, tm, MLA_DKP), BF16),
                      jax.ShapeDtypeStruct((g_, nt, MLA_D_C, tm), BF16)]
        out_specs += [pl.BlockSpec((1, 1, tm, MLA_DKP), lambda g, i: (g, i, 0, 0)),
                      pl.BlockSpec((1, 1, MLA_D_C, tm), lambda g, i: (g, i, 0, 0))]
        body = _mla_prep_kernel
    else:
        body = lambda *a: _mla_prep_kernel(*a, None, None)
    return pl.pallas_call(
        body,
        grid=(g_, nt),
        in_specs=[pspec("mla_cq", tm), pspec("mla_ckv", tm), pspec("misc", tm),
                  pl.BlockSpec((1, 256), c2), pl.BlockSpec((1, MLA_D_C), c2),
                  pl.BlockSpec((256, 384), c2), pl.BlockSpec((384, MLA_HEADS * MLA_DKP), c2),
                  t_spec, t_spec, t_spec, t_spec, t_spec],
        out_specs=out_specs,
        out_shape=out_shape,
        compiler_params=_cparams("parallel", "parallel"),
        name="mla_prep",
    )(proj, proj, proj, qg_p, kvg, wuq_p, wcomb, *tabs)


def _flash_kernel(q_ref, k_ref, vt_ref, o_ref, m_scr, l_scr, acc_scr, *, hpg, dkp, dv, tq, tk):
    qi = pl.program_id(2)
    q = q_ref[0]
    qs = jnp.concatenate([q[:, i * dkp:(i + 1) * dkp] for i in range(hpg)], axis=0)
    r_ = hpg * tq
    m_scr[...] = jnp.full(m_scr.shape, NEG_INF, F32)
    l_scr[...] = jnp.zeros(l_scr.shape, F32)
    acc_scr[...] = jnp.zeros(acc_scr.shape, F32)

    def tile(kj, masked):
        s = _dot_nt(k_ref[0, kj], qs)
        if masked:
            kpos = lax.broadcasted_iota(jnp.int32, (tk, r_), 0) + (kj * tk - qi * tq)
            qpos = lax.broadcasted_iota(jnp.int32, (tk, r_), 1) & (tq - 1)
            s = jnp.where(kpos <= qpos, s, NEG_INF)
        m_old = m_scr[...]
        m_new = jnp.maximum(m_old, jnp.max(s, axis=0, keepdims=True))
        alpha = jnp.exp(m_old - m_new)
        p = jnp.exp(s - m_new)
        l_scr[...] = alpha * l_scr[...] + jnp.sum(p, axis=0, keepdims=True)
        acc_scr[...] = alpha * acc_scr[...] + _dot(vt_ref[0, kj], p.astype(BF16))
        m_scr[...] = m_new

    nfull = (qi * tq) // tk

    def body(kj, c):
        tile(kj, False)
        return c

    lax.fori_loop(0, nfull, body, 0)
    tile(nfull, True)
    o = acc_scr[...] / l_scr[...]
    for i in range(hpg):
        o_ref[0, i * dv:(i + 1) * dv, :] = o[:, i * tq:(i + 1) * tq]


def flash_call(q, k_tiles, vt_tiles, *, groups, hpg, dkp, dv, tq):
    n, s, _ = q.shape
    nk, tk = k_tiles.shape[1], k_tiles.shape[2]
    assert tq & (tq - 1) == 0 and tk % tq == 0 and s % tk == 0
    r_ = hpg * tq
    return pl.pallas_call(
        functools.partial(_flash_kernel, hpg=hpg, dkp=dkp, dv=dv, tq=tq, tk=tk),
        grid=(n, groups, s // tq),
        in_specs=[pl.BlockSpec((1, tq, hpg * dkp), lambda b, g, i: (b, i, g)),
                  pl.BlockSpec((1, nk, tk, dkp), lambda b, g, i: (b, 0, 0, g)),
                  pl.BlockSpec((1, nk, dv, tk), lambda b, g, i: (b, 0, g, 0))],
        out_specs=pl.BlockSpec((1, hpg * dv, tq), lambda b, g, i: (b, g, i)),
        out_shape=jax.ShapeDtypeStruct((n, groups * hpg * dv, s), F32),
        scratch_shapes=[pltpu.VMEM((1, r_), F32), pltpu.VMEM((1, r_), F32), pltpu.VMEM((dv, r_), F32)],
        compiler_params=_cparams("parallel", "parallel", "arbitrary"),
        name="flash",
    )(q, k_tiles, vt_tiles)


FOX_AUG0 = HEAD_DIM
FOX_SCALE = HEAD_DIM ** -0.5


def _log_sigmoid(x):
    return -(jnp.maximum(-x, 0.0) + jnp.log1p(jnp.exp(-jnp.abs(x))))


def _split3(c):
    hi = c.astype(BF16)
    r1 = c - hi.astype(F32)
    mid = r1.astype(BF16)
    lo = (r1 - mid.astype(F32)).astype(BF16)
    return hi, mid, lo


def fox_consts():
    g = FOX_HEADS // FOX_KV_HEADS
    place = np.zeros((384, 128 * (FOX_KV_HEADS + FOX_HEADS)), np.float32)
    const = np.zeros((1, 128 * (FOX_KV_HEADS + FOX_HEADS)), np.float32)
    ones0 = FOX_AUG0 + 3 * g
    for h in range(FOX_HEADS):
        j, gi = divmod(h, g)
        for p in range(3):
            place[128 * p + MISC_F + h, 128 * j + FOX_AUG0 + 3 * gi + p] = 1.0
            place[128 * p + MISC_F + h, 128 * (FOX_KV_HEADS + h) + ones0 + p] = 1.0
            const[0, 128 * (FOX_KV_HEADS + h) + FOX_AUG0 + 3 * gi + p] = -1.0
    for j in range(FOX_KV_HEADS):
        const[0, 128 * j + ones0:128 * j + ones0 + 3] = 1.0
    return jnp.asarray(place, BF16), jnp.asarray(const, F32)


def _fox_prep_kernel(qp_ref, kp_ref, kv_ref, misc_ref, bf_ref, tri_ref, place_ref, const_ref,
                     logf_ref, q_ref, k_ref, vt_ref, carry_scr):
    i = pl.program_id(1)

    @pl.when(i == 0)
    def _():
        carry_scr[...] = jnp.zeros(carry_scr.shape, F32)

    logf = _log_sigmoid(misc_ref[0] + bf_ref[...])
    logf_ref[0] = logf[:, MISC_F:MISC_F + FOX_HEADS]
    c = carry_scr[...] + _dot_hi(tri_ref[...], logf)
    tm = c.shape[0]
    carry_scr[...] = c[tm - 1:tm, :]
    hi, mid, lo = _split3(c)
    aug = _dot(jnp.concatenate([hi, mid, lo], axis=1), place_ref[...]) + const_ref[...]
    nk = 128 * FOX_KV_HEADS
    k_ref[0, 0] = (kp_ref[0] + aug[:, :nk]).astype(BF16)
    q_ref[0] = (qp_ref[0] * FOX_SCALE + aug[:, nk:]).astype(BF16)
    vt_ref[0, 0] = kv_ref[0][:, 128:].T.astype(BF16)


def fox_prep_call(proj, bf_row, tri, place, const, tm):
    g_, r_, _ = proj.shape
    nt = r_ // tm
    c2 = lambda g, i: (0, 0)
    return pl.pallas_call(
        _fox_prep_kernel,
        grid=(g_, nt),
        in_specs=[pspec("fox_qp", tm), pspec("fox_kp", tm), pspec("fox_kv", tm), pspec("misc", tm),
                  pl.BlockSpec((1, 128), c2), pl.BlockSpec((tm, tm), c2),
                  pl.BlockSpec(place.shape, c2), pl.BlockSpec(const.shape, c2)],
        out_specs=[pl.BlockSpec((1, tm, FOX_HEADS), lambda g, i: (g, i, 0)),
                   pl.BlockSpec((1, tm, 128 * FOX_HEADS), lambda g, i: (g, i, 0)),
                   pl.BlockSpec((1, 1, tm, 128 * FOX_KV_HEADS), lambda g, i: (g, i, 0, 0)),
                   pl.BlockSpec((1, 1, 128, tm), lambda g, i: (g, i, 0, 0))],
        out_shape=[jax.ShapeDtypeStruct((g_, r_, FOX_HEADS), F32),
                   jax.ShapeDtypeStruct((g_, r_, 128 * FOX_HEADS), BF16),
                   jax.ShapeDtypeStruct((g_, nt, tm, 128 * FOX_KV_HEADS), BF16),
                   jax.ShapeDtypeStruct((g_, nt, 128, tm), BF16)],
        scratch_shapes=[pltpu.VMEM((1, 128), F32)],
        compiler_params=_cparams("parallel", "arbitrary"),
        name="fox_prep",
    )(proj, proj, proj, proj, bf_row, tri, place, const)


def fox_bf_row(fox_bf_l):
    return jnp.zeros((1, 128), F32).at[0, MISC_F:MISC_F + FOX_HEADS].set(fox_bf_l)


HG_SUB = 16
HG_W = HG_HEADS * HG_DK


def hgrn_consts(t):
    r = np.arange(t)
    same = (r[:, None] // HG_SUB) == (r[None, :] // HG_SUB)
    tri = (same & (r[None, :] <= r[:, None])).astype(np.float32)
    tot = same.astype(np.float32)
    c = np.arange(HG_W)
    head = ((c[:, None] // HG_DK) == (c[None, :] // HG_DK)).astype(np.float32)
    return jnp.asarray(tri), jnp.asarray(tot), jnp.asarray(head, BF16), jnp.asarray(head, F32)


def _hgrn_gates(z, lb):
    f = lb + (1.0 - lb) * _sigmoid(z)
    return jnp.log(f), (1.0 - lb) * _sigmoid(-z)


def _hgrn_kernel(q_ref, f_ref, i_ref, lb_ref, tri_ref, tot_ref, eh_ref, mask_ref, o_ref, st_ref,
                 kk_scr, b_scr, v_scr, od_scr, *, t):
    step = pl.program_id(1)

    @pl.when(step == 0)
    def _():
        st_ref[0] = jnp.zeros(st_ref.shape[1:], F32)

    q = q_ref[0]
    v = i_ref[0]
    logf, kk = _hgrn_gates(f_ref[0], lb_ref[...])
    b = _dot_hi(tri_ref[...], logf)
    btot = _dot_hi(tot_ref[...], logf)
    pad = HG_SUB
    zpad = jnp.zeros((pad, HG_W), F32)
    kk_scr[0:pad, :] = zpad
    b_scr[0:pad, :] = zpad
    v_scr[0:pad, :] = zpad
    kk_scr[pad:pad + t, :] = kk
    b_scr[pad:pad + t, :] = b
    v_scr[pad:pad + t, :] = v
    tmod = lax.broadcasted_iota(jnp.int32, (t, HG_W), 0) & (HG_SUB - 1)
    od = jnp.zeros((t, HG_W), F32)
    for dlt in range(HG_SUB):
        ks = kk_scr[pad - dlt:pad - dlt + t, :]
        bs = b_scr[pad - dlt:pad - dlt + t, :]
        vs = v_scr[pad - dlt:pad - dlt + t, :]
        p = jnp.where(tmod >= dlt, q * ks * jnp.exp(jnp.minimum(b - bs, 0.0)), 0.0)
        od = od + _dot(p.astype(BF16), eh_ref[...]) * vs
    od_scr[...] = od
    qt = (q * jnp.exp(b)).astype(BF16)
    kh = (kk * jnp.exp(btot - b)).astype(BF16)
    dec = jnp.exp(btot)
    vb = v.astype(BF16)
    mask = mask_ref[...]
    st = st_ref[0]
    for c in range(t // HG_SUB):
        r0 = c * HG_SUB
        o_ref[0, r0:r0 + HG_SUB, :] = od_scr[r0:r0 + HG_SUB, :] + _dot_nt(qt[r0:r0 + HG_SUB], st.astype(BF16))
        upd = _dot_tn(vb[r0:r0 + HG_SUB], kh[r0:r0 + HG_SUB])
        st = dec[r0:r0 + 1, :] * st + mask * upd
    st_ref[0] = st


def hgrn_call(proj, lb_row, consts, t):
    n, s, _ = proj.shape
    tri, tot, eh, mask = consts
    c2 = lambda g, i: (0, 0)
    return pl.pallas_call(
        functools.partial(_hgrn_kernel, t=t),
        grid=(n, s // t),
        in_specs=[pspec("hg_q", t), pspec("hg_f", t), pspec("hg_i", t),
                  pl.BlockSpec((1, HG_W), c2), pl.BlockSpec((t, t), c2), pl.BlockSpec((t, t), c2),
                  pl.BlockSpec((HG_W, HG_W), c2), pl.BlockSpec((HG_W, HG_W), c2)],
        out_specs=[pl.BlockSpec((1, t, HG_W), lambda g, i: (g, i, 0)),
                   pl.BlockSpec((1, HG_W, HG_W), lambda g, i: (g, 0, 0))],
        out_shape=[jax.ShapeDtypeStruct((n, s, HG_W), F32), jax.ShapeDtypeStruct((n, HG_W, HG_W), F32)],
        scratch_shapes=[pltpu.VMEM((t + HG_SUB, HG_W), F32), pltpu.VMEM((t + HG_SUB, HG_W), F32),
                        pltpu.VMEM((t + HG_SUB, HG_W), F32), pltpu.VMEM((t, HG_W), F32)],
        compiler_params=_cparams("parallel", "arbitrary"),
        name="hgrn",
    )(proj, proj, proj, lb_row, tri, tot, eh, mask)


NSA_TQ = 128
NSA_R = NSA_HEADS * NSA_TQ
NSA_SCALE = HEAD_DIM ** -0.5


def _nsa_pool_kernel(rows_ref, pool_ref, o_ref, ot_ref):
    x = rows_ref[0][:, :2 * HEAD_DIM].astype(BF16)
    pooled = _dot(pool_ref[...], x)
    o_ref[0] = pooled
    ot_ref[0] = pooled.T


def nsa_pool_call(proj, cmp_w_l):
    n, s, _ = proj.shape
    tm = min(s, 4096)
    nb = tm // L_CMP
    w = jax.nn.softmax(cmp_w_l.astype(F32))
    t = np.arange(tm)
    onehot = jnp.asarray((t[None, :] // L_CMP) == np.arange(nb)[:, None])
    pool = jnp.where(onehot, jnp.tile(w, tm // L_CMP)[None, :], 0.0).astype(BF16)
    return pl.pallas_call(
        _nsa_pool_kernel,
        grid=(n, s // tm),
        in_specs=[pspec("nsa_rows", tm), pl.BlockSpec((nb, tm), lambda g, i: (0, 0))],
        out_specs=[pl.BlockSpec((1, nb, 128), lambda g, i: (g, i, 0)),
                   pl.BlockSpec((1, 128, nb), lambda g, i: (g, 0, i))],
        out_shape=[jax.ShapeDtypeStruct((n, s // L_CMP, 128), F32),
                   jax.ShapeDtypeStruct((n, 128, s // L_CMP), F32)],
        compiler_params=_cparams("parallel", "parallel"),
        name="nsa_pool",
    )(proj, pool)


def _nsa_prep_kernel(k_ref, v_ref, ko_ref, vo_ref, *, nt):
    for j in range(nt):
        ko_ref[0, j] = k_ref[0, j * NSA_TQ:(j + 1) * NSA_TQ, :].astype(BF16)
        vo_ref[0, j] = v_ref[0, j * NSA_TQ:(j + 1) * NSA_TQ, :].T.astype(BF16)


def nsa_prep_call(proj, tm):
    n, s, _ = proj.shape
    nt = tm // NSA_TQ
    shp = jax.ShapeDtypeStruct((n, s // NSA_TQ, NSA_TQ, 128), BF16)
    spec = pl.BlockSpec((1, nt, NSA_TQ, 128), lambda g, i: (g, i, 0, 0))
    return pl.pallas_call(
        functools.partial(_nsa_prep_kernel, nt=nt),
        grid=(n, s // tm),
        in_specs=[pspec("ksw", tm), pspec("vsw", tm)],
        out_specs=[spec, spec],
        out_shape=[shp, shp],
        compiler_params=_cparams("parallel", "parallel"),
        name="nsa_prep",
    )(proj, proj)


def _bias_c_kernel(t5_ref, o_ref):
    qi = pl.program_id(0)
    ncb = o_ref.shape[1]
    blk = lax.broadcasted_iota(jnp.int32, (ncb, NSA_TQ), 0)
    r = lax.broadcasted_iota(jnp.int32, (ncb, NSA_TQ), 1)
    dist = qi * NSA_TQ + r - (blk * L_CMP + L_CMP - 1)
    for h, bias in enumerate(_t5_bias(dist, t5_ref)):
        o_ref[0, :, h * NSA_TQ:(h + 1) * NSA_TQ] = bias


def bias_c_call(t5_table, s):
    nq, ncb = s // NSA_TQ, s // L_CMP
    return pl.pallas_call(
        _bias_c_kernel,
        grid=(nq,),
        in_specs=[pl.BlockSpec(memory_space=pltpu.SMEM)],
        out_specs=pl.BlockSpec((1, ncb, NSA_R), lambda i: (i, 0, 0)),
        out_shape=jax.ShapeDtypeStruct((nq, ncb, NSA_R), F32),
        compiler_params=_cparams("parallel"),
        name="bias_c",
    )(t5_table)


def _bias_tile_kernel(t5_ref, o_ref):
    d = pl.program_id(0)
    c = lax.broadcasted_iota(jnp.int32, (NSA_TQ, NSA_TQ), 0)
    r = lax.broadcasted_iota(jnp.int32, (NSA_TQ, NSA_TQ), 1)
    dist = d * NSA_TQ + r - c
    for h, bias in enumerate(_t5_bias(dist, t5_ref)):
        o_ref[0, :, h * NSA_TQ:(h + 1) * NSA_TQ] = bias


def bias_tile_call(t5_table, nd):
    return pl.pallas_call(
        _bias_tile_kernel,
        grid=(nd,),
        in_specs=[pl.BlockSpec(memory_space=pltpu.SMEM)],
        out_specs=pl.BlockSpec((1, NSA_TQ, NSA_R), lambda i: (i, 0, 0)),
        out_shape=jax.ShapeDtypeStruct((nd, NSA_TQ, NSA_R), F32),
        compiler_params=_cparams("parallel"),
        name="bias_tile",
    )(t5_table)


NSA_WIN_TILES = WINDOW // NSA_TQ + 1


def _online_update(s, mask, vt, m_scr, l_scr, acc_scr):
    if mask is not None:
        s = jnp.where(mask, s, NEG_INF)
    m_old = m_scr[...]
    m_new = jnp.maximum(m_old, jnp.max(s, axis=0, keepdims=True))
    alpha = jnp.exp(m_old - m_new)
    p = jnp.exp(s - m_new)
    if mask is not None:
        p = jnp.where(mask, p, 0.0)
    l_scr[...] = alpha * l_scr[...] + jnp.sum(p, axis=0, keepdims=True)
    acc_scr[...] = alpha * acc_scr[...] + _dot(vt, p.astype(BF16))
    m_scr[...] = m_new


def _nsa_kernel(q_ref, misc_ref, pooled_ref, pooledt_ref, k_ref, vt_ref, bc_ref, bt_ref,
                o_ref, psum_scr, sel_scr, m_s, l_s, acc_s, m_w, l_w, acc_w, *, k_top):
    qi = pl.program_id(1)
    tq, r_ = NSA_TQ, NSA_R
    hd = HEAD_DIM
    q = q_ref[0] * NSA_SCALE
    z64 = jnp.zeros((tq, hd), F32)
    qh = [q[:, hd * h:hd * (h + 1)] for h in range(NSA_HEADS)]
    qs_sel = jnp.concatenate([jnp.concatenate([x, z64], axis=1) for x in qh], axis=0).astype(BF16)
    qs_win = jnp.concatenate([jnp.concatenate([z64, x], axis=1) for x in qh], axis=0).astype(BF16)
    lane_q = lax.broadcasted_iota(jnp.int32, (1, r_), 1) & (tq - 1)
    pos_row = qi * tq + lane_q

    ncb = pooled_ref.shape[1]
    s_c = _dot_nt(pooled_ref[0].astype(BF16), qs_sel) + bc_ref[0]
    blk_c = lax.broadcasted_iota(jnp.int32, (ncb, r_), 0)
    valid_c = blk_c * L_CMP + (L_CMP - 1) <= pos_row
    s_c = jnp.where(valid_c, s_c, NEG_INF)
    m_c = jnp.max(s_c, axis=0, keepdims=True)
    e_c = jnp.where(valid_c, jnp.exp(s_c - m_c), 0.0)
    p_c = e_c / jnp.maximum(jnp.sum(e_c, axis=0, keepdims=True), TINY)
    o_c = _dot(pooledt_ref[0][hd:2 * hd, :].astype(BF16), p_c.astype(BF16))

    psum = p_c[:, 0:tq]
    for h in range(1, NSA_HEADS):
        psum = psum + p_c[:, h * tq:(h + 1) * tq]
    psum_scr[...] = psum
    nsb = ncb // 2
    imp = psum_scr[pl.ds(0, nsb, stride=2), :] + psum_scr[pl.ds(1, nsb, stride=2), :]
    blk = lax.broadcasted_iota(jnp.int32, (nsb, tq), 0)
    cur = (qi * tq + lax.broadcasted_iota(jnp.int32, (nsb, tq), 1)) // L_SEL
    forced = (blk == 0) | (blk == cur) | (blk == cur - 1)
    score = jnp.where(forced, FORCE_SCORE, jnp.where(blk <= cur, imp, -1.0))
    sel = jnp.zeros((nsb, tq), F32)
    for _ in range(k_top):
        mx = jnp.max(score, axis=0, keepdims=True)
        first = jnp.min(jnp.where(score == mx, blk, nsb), axis=0, keepdims=True)
        pick = blk == first
        sel = jnp.where(pick, 1.0, sel)
        score = jnp.where(pick, -jnp.inf, score)
    sel_scr[...] = sel

    for ref, val in ((m_s, NEG_INF), (l_s, 0.0), (acc_s, 0.0), (m_w, NEG_INF), (l_w, 0.0), (acc_w, 0.0)):
        ref[...] = jnp.full(ref.shape, val, F32)
    row = lax.broadcasted_iota(jnp.int32, (tq, tq), 0)
    t31 = bt_ref[NSA_WIN_TILES - 1][0:1, :]
    krow = lax.broadcasted_iota(jnp.int32, (tq, r_), 0)
    causal = krow <= lane_q

    def sel_mask(kj):
        r0 = sel_scr[pl.ds(2 * kj, 1), :]
        r1 = sel_scr[pl.ds(2 * kj + 1, 1), :]
        m1 = jnp.where(row < L_SEL, r0, r1) > 0.0
        return jnp.concatenate([m1] * NSA_HEADS, axis=1)

    def sel_tile(kj, bias, extra):
        s = _dot_nt(k_ref[0, kj], qs_sel) + bias
        mask = sel_mask(kj)
        if extra is not None:
            mask = jnp.logical_and(mask, extra)
        _online_update(s, mask, vt_ref[0, kj][0:hd, :], m_s, l_s, acc_s)

    def far_body(kj, c):
        sel_tile(kj, t31, None)
        return c

    lax.fori_loop(0, jnp.maximum(qi - 1, 0), far_body, 0)

    @pl.when(qi >= 1)
    def _():
        sel_tile(qi - 1, bt_ref[1], None)

    sel_tile(qi, bt_ref[0], causal)

    for d in range(NSA_WIN_TILES - 1, -1, -1):
        def win_tile(d=d):
            kj = qi - d
            s = _dot_nt(k_ref[0, kj], qs_win) + bt_ref[d]
            if d == 0:
                mask = causal
            elif d == NSA_WIN_TILES - 1:
                mask = krow >= lane_q
            else:
                mask = None
            _online_update(s, mask, vt_ref[0, kj][hd:2 * hd, :], m_w, l_w, acc_w)
        if d == 0:
            win_tile()
        else:
            pl.when(qi >= d)(win_tile)

    gt = _sigmoid(misc_ref[0]).T
    def gate(j):
        return jnp.concatenate([gt[MISC_G + 3 * h + j:MISC_G + 3 * h + j + 1, :] for h in range(NSA_HEADS)], axis=1)
    o_s = acc_s[...] / jnp.maximum(l_s[...], TINY)
    o_w = acc_w[...] / jnp.maximum(l_w[...], TINY)
    o = gate(0) * o_c + gate(1) * o_s + gate(2) * o_w
    for h in range(NSA_HEADS):
        o_ref[0, hd * h:hd * (h + 1), :] = o[:, h * tq:(h + 1) * tq]


def nsa_call(proj, pooled, pooled_t, k_tiles, vt_tiles, bias_c, bias_t):
    n, s, _ = proj.shape
    tq = NSA_TQ
    nq, ncb = s // tq, s // L_CMP
    nsb = ncb // 2
    k_top = min(N_SEL, nsb)
    off, w = PROJ["nsa_q"]
    moff, mw = PROJ["misc"]
    return pl.pallas_call(
        functools.partial(_nsa_kernel, k_top=k_top),
        grid=(n, nq),
        in_specs=[pl.BlockSpec((1, tq, w), lambda g, i: (g, i, off // w)),
                  pl.BlockSpec((1, tq, mw), lambda g, i: (g, i, moff // mw)),
                  pl.BlockSpec((1, ncb, 128), lambda g, i: (g, 0, 0)),
                  pl.BlockSpec((1, 128, ncb), lambda g, i: (g, 0, 0)),
                  pl.BlockSpec((1, nq, tq, 128), lambda g, i: (g, 0, 0, 0)),
                  pl.BlockSpec((1, nq, 128, tq), lambda g, i: (g, 0, 0, 0)),
                  pl.BlockSpec((1, ncb, NSA_R), lambda g, i: (i, 0, 0)),
                  pl.BlockSpec((NSA_WIN_TILES, tq, NSA_R), lambda g, i: (0, 0, 0))],
        out_specs=pl.BlockSpec((1, NSA_HEADS * HEAD_DIM, tq), lambda g, i: (g, 0, i)),
        out_shape=jax.ShapeDtypeStruct((n, NSA_HEADS * HEAD_DIM, s), F32),
        scratch_shapes=[pltpu.VMEM((ncb, tq), F32), pltpu.VMEM((nsb, tq), F32),
                        pltpu.VMEM((1, NSA_R), F32), pltpu.VMEM((1, NSA_R), F32), pltpu.VMEM((HEAD_DIM, NSA_R), F32),
                        pltpu.VMEM((1, NSA_R), F32), pltpu.VMEM((1, NSA_R), F32), pltpu.VMEM((HEAD_DIM, NSA_R), F32)],
        compiler_params=_cparams("parallel", "arbitrary"),
        name="nsa",
    )(proj, proj, pooled, pooled_t, k_tiles, vt_tiles, bias_c, bias_t)


def _group_rms(x, g):
    return (x * lax.rsqrt(jnp.mean(x * x, axis=-1, keepdims=True) + RMS_EPS)) * g


def _merge_kernel(x_ref, gt_ref, nsa_ref, hg_ref, fox_ref, lat_ref, hgg_ref, mg_ref, wuv_ref, wo_ref,
                  lg_ref, lb_ref, o_ref, *, alpha, tr):
    if tr:
        o_n = nsa_ref[0].T
        o_f = fox_ref[0].T
        o_m = _dot(wuv_ref[...], lat_ref[0].astype(BF16)).T
    else:
        o_n = nsa_ref[0]
        o_f = fox_ref[0]
        o_m = _dot(lat_ref[0].astype(BF16), wuv_ref[...])
    gw = GROUP_W
    mg = mg_ref[...]
    parts = [_group_rms(o_n, mg[:, 0:gw]),
             _group_rms(hg_ref[0], mg[:, gw:2 * gw]) * _silu(hgg_ref[0]),
             _group_rms(o_f, mg[:, 2 * gw:3 * gw]),
             _group_rms(o_m, mg[:, 3 * gw:4 * gw])]
    merged = jnp.concatenate(parts, axis=1).astype(BF16)
    y = (1.0 + gt_ref[0]) * _dot(merged, wo_ref[...])
    o_ref[0] = _post_ln(x_ref[0], y, lg_ref[...], lb_ref[...], alpha)


def merge_call(x, gt, o_nsa, o_hg, o_fox, o_lat, proj, mix_g_l, wuv, w_out, lg, lb, alpha, tm, tr):
    g_, r_, d = x.shape
    ra = gt.shape[1]
    a_spec = (pl.BlockSpec((1, 1, d), lambda g, i: (g, 0, 0)) if ra == 1
              else pl.BlockSpec((1, tm, d), lambda g, i: (g, i, 0)))
    if tr:
        f_spec = lambda w: pl.BlockSpec((1, w, tm), lambda g, i: (g, 0, i))
    else:
        f_spec = lambda w: pl.BlockSpec((1, tm, w), lambda g, i: (g, i, 0))
    c2 = lambda g, i: (0, 0)
    return pl.pallas_call(
        functools.partial(_merge_kernel, alpha=alpha, tr=tr),
        grid=(g_, r_ // tm),
        in_specs=[pl.BlockSpec((1, tm, d), lambda g, i: (g, i, 0)), a_spec,
                  f_spec(GROUP_W), pl.BlockSpec((1, tm, GROUP_W), lambda g, i: (g, i, 0)), f_spec(GROUP_W),
                  f_spec(MLA_HEADS * MLA_D_C), pspec("hg_g", tm),
                  pl.BlockSpec((1, MIX_W), c2), pl.BlockSpec(wuv.shape, c2), pl.BlockSpec((MIX_W, d), c2),
                  pl.BlockSpec((1, d), c2), pl.BlockSpec((1, d), c2)],
        out_specs=pl.BlockSpec((1, tm, d), lambda g, i: (g, i, 0)),
        out_shape=jax.ShapeDtypeStruct(x.shape, F32),
        compiler_params=_cparams("parallel", "parallel"),
        name="merge",
    )(x, gt, o_nsa, o_hg, o_fox, o_lat, proj, mix_g_l.reshape(1, MIX_W), wuv, w_out,
      lg.reshape(1, d), lb.reshape(1, d))


def mla_wuv(wuv_l, tr):
    wv = wuv_l.reshape(MLA_D_C, MLA_HEADS, MLA_D_V)
    bd = jnp.zeros((MLA_HEADS * MLA_D_C, MLA_HEADS * MLA_D_V), F32)
    for h in range(MLA_HEADS):
        bd = bd.at[MLA_D_C * h:MLA_D_C * (h + 1), MLA_D_V * h:MLA_D_V * (h + 1)].set(wv[:, h, :])
    return (bd.T if tr else bd).astype(BF16)


FOX_T = 256
MLA_T = 128
MLA_TK = 256
HG_T = 256


def prompt_mixer(proj, lw, shared):
    n, s, _ = proj.shape
    pooled, pooled_t = nsa_pool_call(proj, lw["cmp_w"])
    k_tiles, vt_tiles = nsa_prep_call(proj, min(s, 512))
    o_nsa = nsa_call(proj, pooled, pooled_t, k_tiles, vt_tiles, shared["bias_c"], shared["bias_t"])
    o_hg, st_t = hgrn_call(proj, lw["lb_row"], shared["hg_consts"], HG_T)
    logf, fq, fk, fvt = fox_prep_call(proj, lw["fox_bf_row"], shared["fox_tri"], shared["fox_place"],
                                      shared["fox_const"], FOX_T)
    o_fox = flash_call(fq, fk, fvt, groups=FOX_KV_HEADS, hpg=FOX_HEADS // FOX_KV_HEADS, dkp=128,
                       dv=HEAD_DIM, tq=FOX_T)
    mq, mrows, mk, mvt = mla_prep_call(proj, lw["mla_w"], shared["rope_p"], MLA_TK, True)
    o_lat = flash_call(mq, mk, mvt, groups=1, hpg=MLA_HEADS, dkp=MLA_DKP, dv=MLA_D_C, tq=MLA_T)
    return (o_nsa, o_hg, o_fox, o_lat), (logf, mrows, st_t)


def layer_weights(l, w_in, w_out, mix_norm_g, nsa_cmp_w, lb_all, fox_b_f, mla_q_norm_g, mla_kv_norm_g,
                  mla_w_uq, mla_w_uk, mla_w_uv, ffn_w_gate, ffn_w_up, ffn_w_down, ln_g, ln_b):
    return dict(
        w_in=pad_w_in(w_in[l]), w_out=w_out[l].astype(BF16), mix_g=mix_norm_g[l], cmp_w=nsa_cmp_w[l],
        lb_row=lb_all[l].reshape(1, HG_W).astype(F32), fox_bf_row=fox_bf_row(fox_b_f[l]),
        mla_w=mla_weights(mla_q_norm_g[l], mla_kv_norm_g[l], mla_w_uq[l], mla_w_uk[l]),
        wuv_t=mla_wuv(mla_w_uv[l], True), wuv=mla_wuv(mla_w_uv[l], False),
        wg=ffn_w_gate[l].astype(BF16), wu=ffn_w_up[l].astype(BF16), wd=ffn_w_down[l].astype(BF16),
        ln_g=ln_g[l], ln_b=ln_b[l])


def prompt_shared(t5_table, s):
    place, const = fox_consts()
    r = np.arange(FOX_T)
    return dict(
        bias_c=bias_c_call(t5_table, s), bias_t=bias_tile_call(t5_table, NSA_WIN_TILES),
        hg_consts=hgrn_consts(HG_T), fox_tri=jnp.asarray((r[None, :] <= r[:, None]).astype(np.float32)),
        fox_place=place, fox_const=const, rope_p=rope_tables(jnp.arange(s)))


def prompt_layer(x, ada, lw, shared, alpha):
    n, s, d = x.shape
    mod = lambda j, k: ada[:, j, k][:, None, :]
    x = ffn_call(x, mod(0, 0), mod(0, 1), mod(0, 2), lw["wg"][0], lw["wu"][0], lw["wd"][0],
                 lw["ln_g"][0], lw["ln_b"][0], alpha, 512)
    proj = inproj_call(x, mod(1, 0), mod(1, 1), lw["w_in"], 512)
    (o_nsa, o_hg, o_fox, o_lat), (logf, mrows, st_t) = prompt_mixer(proj, lw, shared)
    x = merge_call(x, mod(1, 2), o_nsa, o_hg, o_fox, o_lat, proj, lw["mix_g"], lw["wuv_t"], lw["w_out"],
                   lw["ln_g"][1], lw["ln_b"][1], alpha, 512, True)
    x = ffn_call(x, mod(2, 0), mod(2, 1), mod(2, 2), lw["wg"][1], lw["wu"][1], lw["wd"][1],
                 lw["ln_g"][2], lw["ln_b"][2], alpha, 512)
    o, w = PROJ["nsa_rows"]
    nsa_rows = proj[:, :, o:o + w].reshape(n, s, 4, HEAD_DIM)
    o, w = PROJ["win_rows"]
    new_win = proj[:, s - min(WINDOW, s):, o:o + w].reshape(n, min(WINDOW, s), 2, HEAD_DIM)
    o, w = PROJ["fox_kv"]
    fox_rows = proj[:, :, o:o + w].reshape(n, s, 2, FOX_KV_HEADS, HEAD_DIM)
    return x, (nsa_rows, new_win, fox_rows, logf, mrows, hgrn_state_from_t(st_t))


DEC_PP = 16
QROWS = 8


def _page_specs(pp, block, row_blk):
    return [pl.BlockSpec(block, lambda n, j, pt, _i=i: (pt[n, j * pp + _i], row_blk, 0)) for i in range(pp)]


def _paged_attn_kernel(pt_ref, q_ref, new_ref, *rest, pp, rk, v0, has_bias):
    if has_bias:
        d_ref, rest = rest[0], rest[1:]
    pages, o_ref, m_scr, l_scr, acc_scr = rest[:pp], rest[pp], rest[pp + 1], rest[pp + 2], rest[pp + 3]
    j = pl.program_id(1)
    dv = acc_scr.shape[1]

    @pl.when(j == 0)
    def _():
        m_scr[...] = jnp.full(m_scr.shape, NEG_INF, F32)
        l_scr[...] = jnp.zeros(l_scr.shape, F32)
        acc_scr[...] = jnp.zeros(acc_scr.shape, F32)

    q = q_ref[0]
    kt = jnp.concatenate([p[0, 0:rk, :].astype(BF16) for p in pages], axis=1)
    vt = jnp.concatenate([p[0, v0:v0 + dv, :].astype(BF16) for p in pages], axis=1)
    s = _dot(q, kt)
    if has_bias:
        s = s + jnp.concatenate([d_ref[0, :, i, :] for i in range(pp)], axis=1)
    m_old = m_scr[...]
    m_new = jnp.maximum(m_old, jnp.max(s, axis=1, keepdims=True))
    alpha = jnp.exp(m_old - m_new)
    p = jnp.exp(s - m_new)
    l_scr[...] = alpha * l_scr[...] + jnp.sum(p, axis=1, keepdims=True)
    acc_scr[...] = alpha * acc_scr[...] + _dot_nt(p.astype(BF16), vt)
    m_scr[...] = m_new

    @pl.when(j == pl.num_programs(1) - 1)
    def _():
        new = new_ref[0].astype(BF16).astype(F32)
        s_new = jnp.sum(q.astype(F32) * new[:, 0:rk], axis=1, keepdims=True)
        m_old = m_scr[...]
        m_f = jnp.maximum(m_old, s_new)
        a = jnp.exp(m_old - m_f)
        p_new = jnp.exp(s_new - m_f)
        l_f = a * l_scr[...] + p_new
        o_ref[0] = (a * acc_scr[...] + p_new.astype(BF16).astype(F32) * new[:, v0:v0 + dv]) / l_f


def paged_attn_call(page_table, pool_t, q8, new_rows, bias, rk, v0, dv):
    n, npages = page_table.shape
    r = pool_t.shape[1]
    pp = DEC_PP
    in_specs = [pl.BlockSpec((1, QROWS, rk), lambda b, j, pt: (b, 0, 0)),
                pl.BlockSpec((1, 1, r), lambda b, j, pt: (b, 0, 0))]
    args = [q8, new_rows]
    if bias is not None:
        in_specs.append(pl.BlockSpec((1, QROWS, pp, PAGE_SIZE), lambda b, j, pt: (b, 0, j, 0)))
        args.append(bias)
    in_specs += _page_specs(pp, (1, r, PAGE_SIZE), 0)
    args += [pool_t] * pp
    return pl.pallas_call(
        functools.partial(_paged_attn_kernel, pp=pp, rk=rk, v0=v0, has_bias=bias is not None),
        grid_spec=pltpu.PrefetchScalarGridSpec(
            num_scalar_prefetch=1, grid=(n, npages // pp), in_specs=in_specs,
            out_specs=pl.BlockSpec((1, QROWS, dv), lambda b, j, pt: (b, 0, 0)),
            scratch_shapes=[pltpu.VMEM((QROWS, 1), F32), pltpu.VMEM((QROWS, 1), F32), pltpu.VMEM((QROWS, dv), F32)]),
        out_shape=jax.ShapeDtypeStruct((n, QROWS, dv), F32),
        compiler_params=_cparams("parallel", "arbitrary"),
        name="paged_attn",
    )(page_table, *args)


def _fox_dec_bias_kernel(lg_ref, misc_ref, bf_ref, after_ref, later_ref, d_ref, logf_ref):
    logf_new = _log_sigmoid(misc_ref[0] + bf_ref[...])
    logf_ref[0] = logf_new
    ones = jnp.ones((PAGE_SIZE, PAGE_SIZE), F32)
    for h in range(FOX_HEADS):
        x = lg_ref[0, h]
        within = _dot_hi(x, after_ref[...])
        across = _dot_hi(later_ref[...], _dot_hi(x, ones))
        d_ref[0, h] = within + across + logf_new[:, MISC_F + h:MISC_F + h + 1]
    for h in range(FOX_HEADS, QROWS):
        d_ref[0, h] = jnp.zeros(d_ref.shape[2:], F32)


def fox_dec_bias_call(logf_pages, misc_rows, bf_row):
    n, _, npages, _ = logf_pages.shape
    pos = np.arange(PAGE_SIZE)
    after = jnp.asarray((pos[:, None] > pos[None, :]).astype(np.float32))
    pg = np.arange(npages)
    later = jnp.asarray((pg[None, :] > pg[:, None]).astype(np.float32))
    return pl.pallas_call(
        _fox_dec_bias_kernel,
        grid=(n,),
        in_specs=[pl.BlockSpec((1, FOX_HEADS, npages, PAGE_SIZE), lambda b: (b, 0, 0, 0)),
                  pl.BlockSpec((1, 1, 128), lambda b: (b, 0, 0)),
                  pl.BlockSpec((1, 128), lambda b: (0, 0)),
                  pl.BlockSpec((PAGE_SIZE, PAGE_SIZE), lambda b: (0, 0)),
                  pl.BlockSpec((npages, npages), lambda b: (0, 0))],
        out_specs=[pl.BlockSpec((1, QROWS, npages, PAGE_SIZE), lambda b: (b, 0, 0, 0)),
                   pl.BlockSpec((1, 1, 128), lambda b: (b, 0, 0))],
        out_shape=[jax.ShapeDtypeStruct((n, QROWS, npages, PAGE_SIZE), F32),
                   jax.ShapeDtypeStruct((n, 1, 128), F32)],
        compiler_params=_cparams("parallel"),
        name="fox_dec_bias",
    )(logf_pages, misc_rows, bf_row, after, later)


NSA_POOL_PP = 32


def _nsa_pool_dec_kernel(pt_ref, pool_ref, *rest, pp):
    pages, o_ref = rest[:pp], rest[pp]
    x = jnp.concatenate([p[0].astype(BF16) for p in pages], axis=1)
    o_ref[0] = _dot(x, pool_ref[...])


def nsa_pool_dec_call(page_table, pool_t, cmp_w_l):
    n, npages = page_table.shape
    pp = min(NSA_POOL_PP, npages)
    tm = pp * PAGE_SIZE
    nb = tm // L_CMP
    w = jax.nn.softmax(cmp_w_l.astype(F32))
    t = np.arange(tm)
    onehot = jnp.asarray((t[:, None] // L_CMP) == np.arange(nb)[None, :])
    pmat = jnp.where(onehot, jnp.tile(w, tm // L_CMP)[:, None], 0.0).astype(BF16)
    return pl.pallas_call(
        functools.partial(_nsa_pool_dec_kernel, pp=pp),
        grid_spec=pltpu.PrefetchScalarGridSpec(
            num_scalar_prefetch=1, grid=(n, npages // pp),
            in_specs=[pl.BlockSpec((tm, nb), lambda b, j, pt: (0, 0))] + _page_specs(pp, (1, 128, PAGE_SIZE), 0),
            out_specs=pl.BlockSpec((1, 128, nb), lambda b, j, pt: (b, 0, j))),
        out_shape=jax.ShapeDtypeStruct((n, 128, npages * PAGE_SIZE // L_CMP), F32),
        compiler_params=_cparams("parallel", "parallel"),
        name="nsa_pool_dec",
    )(page_table, pmat, *([pool_t] * pp))


assert T5_THR[-1] <= PAGE_SIZE


def _bias_rows_kernel(t5_ref, oc_ref, ow_ref, od_ref, *, t_len, wb):
    ncb = oc_ref.shape[1]
    dist_c = t_len - (lax.broadcasted_iota(jnp.int32, (1, ncb), 1) * L_CMP + L_CMP - 1)
    dist_w = wb - lax.broadcasted_iota(jnp.int32, (1, wb), 1)
    lane = lax.broadcasted_iota(jnp.int32, (1, 2 * PAGE_SIZE), 1)
    dist_d = jnp.where(lane < PAGE_SIZE, PAGE_SIZE - lane, lane - PAGE_SIZE)
    for ref, dist in ((oc_ref, dist_c), (ow_ref, dist_w), (od_ref, dist_d)):
        ref[...] = jnp.zeros(ref.shape, F32)
        for h, bias in enumerate(_t5_bias(dist, t5_ref)):
            ref[h:h + 1, :] = bias


def bias_rows_call(t5_table, t_len, wb):
    ncb = t_len // L_CMP
    return pl.pallas_call(
        functools.partial(_bias_rows_kernel, t_len=t_len, wb=wb),
        in_specs=[pl.BlockSpec(memory_space=pltpu.SMEM)],
        out_shape=[jax.ShapeDtypeStruct((QROWS, ncb), F32), jax.ShapeDtypeStruct((QROWS, wb), F32),
                   jax.ShapeDtypeStruct((QROWS, 2 * PAGE_SIZE), F32)],
        name="bias_rows",
    )(t5_table)


def _nsa_cmp_dec_kernel(q_ref, pooled_ref, bc_ref, pair_ref, oc_ref, imp_ref):
    kvt = pooled_ref[0].astype(BF16)
    s = _dot(q_ref[0], kvt[0:HEAD_DIM, :]) + bc_ref[...]
    m = jnp.max(s, axis=1, keepdims=True)
    e = jnp.exp(s - m)
    p = e / jnp.maximum(jnp.sum(e, axis=1, keepdims=True), TINY)
    oc_ref[0] = _dot_nt(p.astype(BF16), kvt[HEAD_DIM:2 * HEAD_DIM, :])
    head = lax.broadcasted_iota(jnp.int32, p.shape, 0) < NSA_HEADS
    psum = jnp.sum(jnp.where(head, p, 0.0), axis=0, keepdims=True)
    imp_ref[0] = _dot_hi(jnp.broadcast_to(psum, (QROWS, psum.shape[1])), pair_ref[...])[0:1, :]


def nsa_cmp_dec_call(q8, pooled_t, bias_c):
    n, _, ncb = pooled_t.shape
    b = np.arange(ncb)
    pair = jnp.asarray((b[:, None] // 2 == np.arange(ncb // 2)[None, :]).astype(np.float32))
    return pl.pallas_call(
        _nsa_cmp_dec_kernel,
        grid=(n,),
        in_specs=[pl.BlockSpec((1, QROWS, HEAD_DIM), lambda i: (i, 0, 0)),
                  pl.BlockSpec((1, 128, ncb), lambda i: (i, 0, 0)),
                  pl.BlockSpec((QROWS, ncb), lambda i: (0, 0)),
                  pl.BlockSpec((ncb, ncb // 2), lambda i: (0, 0))],
        out_specs=[pl.BlockSpec((1, QROWS, HEAD_DIM), lambda i: (i, 0, 0)),
                   pl.BlockSpec((1, 1, ncb // 2), lambda i: (i, 0, 0))],
        out_shape=[jax.ShapeDtypeStruct((n, QROWS, HEAD_DIM), F32), jax.ShapeDtypeStruct((n, 1, ncb // 2), F32)],
        compiler_params=_cparams("parallel"),
        name="nsa_cmp_dec",
    )(q8, pooled_t, bias_c, pair)


def _topk_dec_kernel(imp_ref, idx_ref, *, t_len, k_top):
    imp_t = imp_ref[...].T
    nfull, n = imp_t.shape
    rows = idx_ref.shape[0]
    cur = t_len // L_SEL
    score = jnp.concatenate([imp_t, jnp.zeros((SUBLANES, n), F32)], axis=0)
    blk = lax.broadcasted_iota(jnp.int32, score.shape, 0)
    forced = (blk == 0) | (blk == cur) | (blk == cur - 1)
    score = jnp.where(forced, FORCE_SCORE, jnp.where(blk <= cur, score, -jnp.inf))
    big = nfull + SUBLANES
    picks = []
    for _ in range(k_top):
        mx = jnp.max(score, axis=0, keepdims=True)
        first = jnp.min(jnp.where(score == mx, blk, big), axis=0, keepdims=True)
        picks.append(first)
        score = jnp.where(blk == first, -jnp.inf, score)
    picks += [jnp.zeros((1, n), jnp.int32)] * (rows - k_top)
    idx_ref[...] = jnp.concatenate(picks, axis=0)


def topk_dec_call(imp, t_len):
    n = imp.shape[0]
    nsb = -(-(t_len + 1) // L_SEL)
    k_top = min(N_SEL, nsb)
    return pl.pallas_call(
        functools.partial(_topk_dec_kernel, t_len=t_len, k_top=k_top),
        out_shape=jax.ShapeDtypeStruct((N_SEL, n), jnp.int32),
        name="topk_dec",
    )(imp)


def _nsa_selwin_dec_kernel(pt_ref, idx_ref, q_ref, g_ref, oc_ref, snew_ref, wnew_ref, win_ref, bw_ref, bd_ref,
                           *rest, t_len, k_top):
    pages, o_ref = rest[:N_SEL], rest[N_SEL]
    b = pl.program_id(0)
    hd = HEAD_DIM
    q = q_ref[0]
    qf = q.astype(F32)
    cur = t_len // L_SEL
    last_page = t_len // PAGE_SIZE - 1
    bd = bd_ref[...]
    near, far, bias0 = bd[:, 0:PAGE_SIZE], bd[:, 0:1], bd[:, PAGE_SIZE:PAGE_SIZE + 1]
    kt = jnp.concatenate([pg[0, 0:hd, :].astype(BF16) for pg in pages], axis=1)
    vt = jnp.concatenate([pg[0, hd:2 * hd, :].astype(BF16) for pg in pages], axis=1)
    nk = N_SEL * PAGE_SIZE
    lane = lax.broadcasted_iota(jnp.int32, (1, nk), 1)
    slot = lane // PAGE_SIZE
    in_page = lane & (PAGE_SIZE - 1)
    blk = jnp.full((1, nk), -1, jnp.int32)
    biases = []
    far_tile = jnp.broadcast_to(far, (QROWS, PAGE_SIZE))
    for i in range(N_SEL):
        if i < k_top:
            bi = idx_ref[b * N_SEL + i]
            blk = jnp.where(slot == i, bi, blk)
            biases.append(jnp.where(bi // 2 == last_page, near, far_tile))
        else:
            biases.append(far_tile)
    bias = jnp.concatenate(biases, axis=1)
    valid = (blk >= 0) & (blk < cur) & ((in_page // L_SEL) == (blk & 1))
    s = jnp.where(valid, _dot(q, kt) + bias, NEG_INF)
    snew = snew_ref[0].astype(BF16).astype(F32)
    s_new = jnp.sum(qf * snew[:, 0:hd], axis=1, keepdims=True) + bias0
    m = jnp.maximum(jnp.max(s, axis=1, keepdims=True), s_new)
    e = jnp.where(valid, jnp.exp(s - m), 0.0)
    e_new = jnp.exp(s_new - m)
    den = jnp.maximum(jnp.sum(e, axis=1, keepdims=True) + e_new, TINY)
    o_s = (_dot_nt(e.astype(BF16), vt) + e_new.astype(BF16).astype(F32) * snew[:, hd:2 * hd]) / den
    winb = win_ref[0].astype(BF16)
    wnb = wnew_ref[0].astype(BF16).astype(F32)
    s_w = _dot(q, winb[0:hd, :]) + bw_ref[...]
    sw_new = jnp.sum(qf * wnb[:, 0:hd], axis=1, keepdims=True) + bias0
    m_w = jnp.maximum(jnp.max(s_w, axis=1, keepdims=True), sw_new)
    e_w = jnp.exp(s_w - m_w)
    ew_new = jnp.exp(sw_new - m_w)
    den_w = jnp.maximum(jnp.sum(e_w, axis=1, keepdims=True) + ew_new, TINY)
    o_w = (_dot_nt(e_w.astype(BF16), winb[hd:2 * hd, :]) + ew_new.astype(BF16).astype(F32) * wnb[:, hd:2 * hd]) / den_w
    g = _sigmoid(g_ref[0])
    o_ref[0] = g[:, 0:1] * oc_ref[0] + g[:, 1:2] * o_s + g[:, 2:3] * o_w


def nsa_selwin_dec_call(page_table, idx, pool_t, q8, g8, o_c, sel_new, win_new, win_t, bias_w, bias_d, t_len):
    n, npages = page_table.shape
    wb = win_t.shape[2]
    nsb = -(-(t_len + 1) // L_SEL)
    k_top = min(N_SEL, nsb)
    row = lambda w: pl.BlockSpec((1, 1, w), lambda b, pt, ix: (b, 0, 0))
    q_spec = pl.BlockSpec((1, QROWS, HEAD_DIM), lambda b, pt, ix: (b, 0, 0))
    pg_specs = [pl.BlockSpec((1, 128, PAGE_SIZE),
                             lambda b, pt, ix, _i=i: (pt[b, jnp.clip(ix[b * N_SEL + _i] // 2, 0, npages - 1)], 1, 0))
                for i in range(N_SEL)]
    return pl.pallas_call(
        functools.partial(_nsa_selwin_dec_kernel, t_len=t_len, k_top=k_top),
        grid_spec=pltpu.PrefetchScalarGridSpec(
            num_scalar_prefetch=2, grid=(n,),
            in_specs=[q_spec, pl.BlockSpec((1, QROWS, 128), lambda b, pt, ix: (b, 0, 0)), q_spec, row(128), row(128),
                      pl.BlockSpec((1, 128, wb), lambda b, pt, ix: (b, 0, 0)),
                      pl.BlockSpec((QROWS, wb), lambda b, pt, ix: (0, 0)),
                      pl.BlockSpec((QROWS, 2 * PAGE_SIZE), lambda b, pt, ix: (0, 0))] + pg_specs,
            out_specs=q_spec),
        out_shape=jax.ShapeDtypeStruct((n, QROWS, HEAD_DIM), F32),
        compiler_params=_cparams("parallel"),
        name="nsa_selwin_dec",
    )(page_table, idx, q8, g8, o_c, sel_new, win_new, win_t, bias_w, bias_d, *([pool_t] * N_SEL))


def _hgrn_dec_kernel(q_ref, z_ref, v_ref, lb_ref, s_ref, o_ref, so_ref):
    logf, kk = _hgrn_gates(z_ref[0], lb_ref[0])
    f = jnp.exp(logf)
    q = q_ref[0]
    v = v_ref[0]
    acc = jnp.zeros(v.shape, F32)
    for k in range(HG_DK):
        s1 = f[k:k + 1, :] * s_ref[0, k] + kk[k:k + 1, :] * v
        so_ref[0, k] = s1
        acc = acc + q[k:k + 1, :] * s1
    o_ref[0] = acc


def hgrn_dec_call(q_t, z_t, v_t, lb_t, state_t):
    h, dk, dv, n = state_t.shape
    r3 = lambda a: pl.BlockSpec((1, a, n), lambda i: (i, 0, 0))
    s_spec = pl.BlockSpec((1, dk, dv, n), lambda i: (i, 0, 0, 0))
    return pl.pallas_call(
        _hgrn_dec_kernel,
        grid=(h,),
        in_specs=[r3(dk), r3(dk), r3(dv), r3(dk), s_spec],
        out_specs=[r3(dv), s_spec],
        out_shape=[jax.ShapeDtypeStruct((h, dv, n), F32), jax.ShapeDtypeStruct(state_t.shape, F32)],
        compiler_params=_cparams("parallel"),
        name="hgrn_dec",
    )(q_t, z_t, v_t, lb_t, state_t)


def _pad_rows(x, rows=QROWS):
    return jnp.pad(x, ((0, 0), (0, rows - x.shape[1]), (0, 0)))


def decode_layer(x, ada, lw, shared, caches, page_table, t5_table, alpha):
    _, n, d = x.shape
    c_nsa, c_win, c_fox, c_logf, c_mla, c_hg = caches
    n_phys = c_nsa.shape[0]
    npages = page_table.shape[1]
    t_len = npages * PAGE_SIZE
    mod = lambda j, k: ada[:, j, k][None]
    x = ffn_call(x, mod(0, 0), mod(0, 1), mod(0, 2), lw["wg"][0], lw["wu"][0], lw["wd"][0],
                 lw["ln_g"][0], lw["ln_b"][0], alpha, n)
    proj = inproj_call(x, mod(1, 0), mod(1, 1), lw["w_in"], n)
    p2 = proj[0]
    col = lambda name: p2[:, PROJ[name][0]:PROJ[name][0] + PROJ[name][1]]
    misc = col("misc")
    mq, mrows = mla_prep_call(proj, lw["mla_w"], shared["rope_d"], n, False)
    dk_m = MLA_D_C + MLA_D_ROPE
    q8 = _pad_rows(mq[0].reshape(n, MLA_HEADS, MLA_DKP)[:, :, :dk_m])
    mla_t = jnp.swapaxes(c_mla, 1, 2)
    o_lat = paged_attn_call(page_table, mla_t, q8, mrows[0][:, None, :], None, dk_m, 0, MLA_D_C)
    o_lat = o_lat[:, :MLA_HEADS].reshape(1, n, MLA_HEADS * MLA_D_C)
    g = FOX_HEADS // FOX_KV_HEADS
    fq = col("fox_qp").reshape(n, FOX_HEADS, 128)[:, :, :HEAD_DIM] * FOX_SCALE
    q_bd = jnp.zeros((n, FOX_HEADS, 2 * HEAD_DIM), F32)
    for h in range(FOX_HEADS):
        q_bd = q_bd.at[:, h, HEAD_DIM * (h // g):HEAD_DIM * (h // g + 1)].set(fq[:, h])
    logf_pages = jnp.swapaxes(jnp.swapaxes(c_logf, 1, 2)[page_table], 1, 2)
    fbias, logf_row = fox_dec_bias_call(logf_pages, misc[:, None, :], lw["fox_bf_row"])
    fox_t = jnp.transpose(c_fox, (0, 2, 3, 4, 1)).reshape(n_phys, 4 * HEAD_DIM, PAGE_SIZE)
    o_fox = paged_attn_call(page_table, fox_t, _pad_rows(q_bd).astype(BF16), col("fox_kv")[:, None, :], fbias,
                            2 * HEAD_DIM, 2 * HEAD_DIM, 2 * HEAD_DIM)
    o_fox = jnp.concatenate([o_fox[:, h, HEAD_DIM * (h // g):HEAD_DIM * (h // g + 1)]
                             for h in range(FOX_HEADS)], axis=-1)[None]
    nsa_t = jnp.transpose(c_nsa, (0, 2, 3, 1)).reshape(n_phys, 4 * HEAD_DIM, PAGE_SIZE)
    pooled_t = nsa_pool_dec_call(page_table, nsa_t, lw["cmp_w"])
    qn8 = _pad_rows(col("nsa_q").reshape(n, NSA_HEADS, HEAD_DIM) * NSA_SCALE).astype(BF16)
    o_c, imp = nsa_cmp_dec_call(qn8, pooled_t, shared["bias_c_d"])
    idx = topk_dec_call(imp[:, 0, :], t_len).T.reshape(-1)
    g8 = _pad_rows(jnp.pad(misc[:, MISC_G:MISC_G + 3 * NSA_HEADS].reshape(n, NSA_HEADS, 3),
                           ((0, 0), (0, 0), (0, 125))))
    wb = c_win.shape[1]
    win_t = jnp.transpose(c_win, (0, 2, 3, 1)).reshape(n, 2 * HEAD_DIM, wb)
    win_rows = col("win_rows")
    o_nsa = nsa_selwin_dec_call(page_table, idx, nsa_t, qn8, g8, o_c, col("nsa_rows")[:, None, 2 * HEAD_DIM:],
                                win_rows[:, None, :], win_t, shared["bias_w_d"], shared["bias_d_d"], t_len)
    o_nsa = o_nsa[:, :NSA_HEADS].reshape(1, n, GROUP_W)
    win_out = jnp.concatenate([c_win[:, 1:], win_rows.reshape(n, 1, 2, HEAD_DIM)], axis=1)
    tr = lambda name: col(name).T.reshape(HG_HEADS, HG_DK, n)
    lb_t = jnp.broadcast_to(lw["lb_row"].reshape(HG_HEADS, HG_DK, 1), (HG_HEADS, HG_DK, n))
    o_hg, hg_new = hgrn_dec_call(tr("hg_q"), tr("hg_f"), tr("hg_i"), lb_t,
                                 jnp.transpose(c_hg.astype(F32), (1, 2, 3, 0)))
    o_hg = o_hg.reshape(GROUP_W, n).T[None]
    x = merge_call(x, mod(1, 2), o_nsa, o_hg, o_fox, o_lat, proj, lw["mix_g"], lw["wuv"], lw["w_out"],
                   lw["ln_g"][1], lw["ln_b"][1], alpha, n, False)
    x = ffn_call(x, mod(2, 0), mod(2, 1), mod(2, 2), lw["wg"][1], lw["wu"][1], lw["wd"][1],
                 lw["ln_g"][2], lw["ln_b"][2], alpha, n)
    st = (col("nsa_rows").reshape(n, 1, 4, HEAD_DIM), win_out,
          col("fox_kv").reshape(n, 1, 2, FOX_KV_HEADS, HEAD_DIM),
          logf_row[:, :, MISC_F:MISC_F + FOX_HEADS], mrows[0][:, None, :],
          jnp.transpose(hg_new, (3, 0, 1, 2)))
    return x, st


def kernel(x_prompt, x_sample, c_prompt, c_sample, page_table, cache_nsa_kv, cache_nsa_win, cache_fox_kv,
           cache_fox_logf, cache_mla, state_hgrn, w_in, w_out, mix_norm_g, nsa_cmp_w, t5_table, hgrn_lb_logits,
           fox_b_f, mla_q_norm_g, mla_kv_norm_g, mla_w_uq, mla_w_uk, mla_w_uv, ffn_w_gate, ffn_w_up, ffn_w_down,
           ada_w, ada_b, ln_g, ln_b):
    depth = w_in.shape[0]
    alpha = (2 * depth) ** 0.25
    nb, s, d = x_prompt.shape
    nd = x_sample.shape[0]
    assert x_sample.shape[1] == 1
    t_len = page_table.shape[1] * PAGE_SIZE
    lb_p = jax.nn.softmax(hgrn_lb_logits.astype(F32), axis=0)
    lb_all = jnp.cumsum(lb_p, axis=0) - lb_p
    shared = prompt_shared(t5_table, s)
    bias_c_d, bias_w_d, bias_d_d = bias_rows_call(t5_table, t_len, cache_nsa_win.shape[2])
    shared.update(bias_c_d=bias_c_d, bias_w_d=bias_w_d, bias_d_d=bias_d_d,
                  rope_d=rope_tables(jnp.full((1,), t_len, jnp.int32)))
    c_all = jnp.concatenate([c_prompt, c_sample], axis=0)
    c_all = jnp.pad(c_all, ((0, -c_all.shape[0] % SUBLANES), (0, 0)))
    xp, xs = x_prompt, x_sample.reshape(1, nd, d)
    st_p, st_s = [], []
    for l in range(depth):
        lw = layer_weights(l, w_in, w_out, mix_norm_g, nsa_cmp_w, lb_all, fox_b_f, mla_q_norm_g, mla_kv_norm_g,
                           mla_w_uq, mla_w_uk, mla_w_uv, ffn_w_gate, ffn_w_up, ffn_w_down, ln_g, ln_b)
        ada = ada_call(c_all, ada_w[l], ada_b[l]).reshape(-1, N_SUB, 3, d)
        xp, sp = prompt_layer(xp, ada[:nb], lw, shared, alpha)
        caches = (cache_nsa_kv[l], cache_nsa_win[l], cache_fox_kv[l], cache_fox_logf[l], cache_mla[l], state_hgrn[l])
        xs, ss = decode_layer(xs, ada[nb:nb + nd], lw, shared, caches, page_table, t5_table, alpha)
        st_p.append(sp)
        st_s.append(ss)
    outs = [xp, xs.reshape(nd, 1, d)]
    for group in (st_p, st_s):
        outs += [jnp.stack([t[i] for t in group]) for i in range(6)]
    return tuple(outs)


def hgrn_state_from_t(st_t):
    n = st_t.shape[0]
    blk = st_t.reshape(n, HG_HEADS, HG_DV, HG_HEADS, HG_DK)
    diag = jnp.stack([blk[:, h, :, h, :] for h in range(HG_HEADS)], axis=1)
    return jnp.swapaxes(diag, -1, -2)
```

```python
import functools
import math

import jax
import jax.numpy as jnp
import numpy as np
from jax import lax
from jax.experimental import pallas as pl
from jax.experimental.pallas import tpu as pltpu

F32 = jnp.float32
BF16 = jnp.bfloat16

HEAD_DIM = 64
N_GROUPS = 4
GROUP_W = 256
MIX_W = N_GROUPS * GROUP_W
NSA_HEADS = 4
L_CMP = 32
L_SEL = 64
N_SEL = 16
WINDOW = 512
FORCE_SCORE = 1.0e4
HG_HEADS = 4
HG_DK = 64
HG_DV = 64
FOX_HEADS = 4
FOX_KV_HEADS = 2
MLA_HEADS = 4
MLA_D_CQ = 192
MLA_D_C = 128
MLA_D_NOPE = 64
MLA_D_ROPE = 32
MLA_D_V = 64
ROPE_BASE = 10000.0
T5_BUCKETS = 32
T5_MAX_EXACT = 16
T5_MAX_DIST = 128
N_SUB = 3
LN_EPS = 1e-5
RMS_EPS = 1e-6
NEG_INF = -1e30
TINY = 1e-30
PAGE_SIZE = 128

LANES = 128
SUBLANES = 8
VMEM_LIMIT = 56 * 1024 * 1024

IN_SIZES = (NSA_HEADS * HEAD_DIM, 6 * HEAD_DIM, 3 * NSA_HEADS,
            HG_HEADS * HG_DK, HG_HEADS * HG_DK, HG_HEADS * HG_DV, HG_HEADS * HG_DV,
            FOX_HEADS * HEAD_DIM, FOX_KV_HEADS * HEAD_DIM, FOX_KV_HEADS * HEAD_DIM, FOX_HEADS,
            MLA_D_CQ, MLA_D_C, MLA_D_ROPE)
IN_OFFS = tuple(int(v) for v in np.cumsum((0,) + IN_SIZES))


def _t5_thresholds():
    d = np.arange(0, 4 * T5_MAX_DIST)
    nf = np.maximum(d, 1).astype(np.float64)
    large = T5_MAX_EXACT + (np.log(nf / T5_MAX_EXACT) / math.log(T5_MAX_DIST / T5_MAX_EXACT)
                            * (T5_BUCKETS - T5_MAX_EXACT)).astype(np.int64)
    bucket = np.where(d < T5_MAX_EXACT, d, np.minimum(large, T5_BUCKETS - 1))
    return tuple(int(np.argmax(bucket >= b)) for b in range(1, T5_BUCKETS))


T5_THR = _t5_thresholds()


def _cparams(*sem):
    return pltpu.CompilerParams(dimension_semantics=sem, vmem_limit_bytes=VMEM_LIMIT)


def _dot(a, b):
    return jnp.dot(a, b, preferred_element_type=F32)


def _dot_nt(a, b):
    return lax.dot_general(a, b, (((1,), (1,)), ((), ())), preferred_element_type=F32)


def _dot_tn(a, b):
    return lax.dot_general(a, b, (((0,), (0,)), ((), ())), preferred_element_type=F32)


def _dot_hi(a, b):
    return jnp.dot(a, b, preferred_element_type=F32, precision=lax.Precision.HIGHEST)


def _sigmoid(x):
    return 1.0 / (1.0 + jnp.exp(-x))


def _silu(x):
    return x * _sigmoid(x)


def _t5_bias(dist, t5_ref, heads=NSA_HEADS):
    outs = [jnp.full(dist.shape, t5_ref[0, h], F32) for h in range(heads)]
    for b in range(1, T5_BUCKETS):
        ge = dist >= T5_THR[b - 1]
        outs = [jnp.where(ge, t5_ref[b, h], o) for h, o in enumerate(outs)]
    return outs


def _ada_kernel(c_ref, w_ref, b_ref, o_ref):
    c = _silu(c_ref[...]).astype(BF16)
    o_ref[...] = _dot(c, w_ref[...].astype(BF16)) + b_ref[...]


def ada_call(c, w, b):
    m, d = c.shape
    n = w.shape[1]
    tn = 1152 if n % 1152 == 0 else n
    return pl.pallas_call(
        _ada_kernel,
        grid=(n // tn,),
        in_specs=[pl.BlockSpec((m, d), lambda j: (0, 0)),
                  pl.BlockSpec((d, tn), lambda j: (0, j)),
                  pl.BlockSpec((1, tn), lambda j: (0, j))],
        out_specs=pl.BlockSpec((m, tn), lambda j: (0, j)),
        out_shape=jax.ShapeDtypeStruct((m, n), F32),
        compiler_params=_cparams("arbitrary"),
        name="ada",
    )(c, w, b.reshape(1, n))


def _post_ln(x, y, g, b, alpha):
    z = alpha * x + y
    mu = jnp.mean(z, axis=-1, keepdims=True)
    zc = z - mu
    var = jnp.mean(zc * zc, axis=-1, keepdims=True)
    return zc * lax.rsqrt(var + LN_EPS) * g + b


FF_CHUNK = 256


def _ffn_kernel(x_ref, sh_ref, sc_ref, gt_ref, wg_ref, wu_ref, wd_ref, lg_ref, lb_ref, o_ref, *, alpha):
    x = x_ref[0]
    h = (x * (1.0 + sc_ref[0]) + sh_ref[0]).astype(BF16)
    d_ff = wg_ref.shape[1]
    acc = jnp.zeros(x.shape, F32)
    for c0 in range(0, d_ff, FF_CHUNK):
        g = _dot(h, wg_ref[:, c0:c0 + FF_CHUNK])
        u = _dot(h, wu_ref[:, c0:c0 + FF_CHUNK])
        a = (_silu(g) * u).astype(BF16)
        acc = acc + _dot(a, wd_ref[c0:c0 + FF_CHUNK, :])
    y = 0.5 * (1.0 + gt_ref[0]) * acc
    o_ref[0] = _post_ln(x, y, lg_ref[...], lb_ref[...], alpha)


def ffn_call(x, sh, sc, gt, wg, wu, wd, lg, lb, alpha, tm):
    g_, r_, d = x.shape
    ra = sh.shape[1]
    d_ff = wg.shape[1]
    assert d_ff % FF_CHUNK == 0 and r_ % tm == 0
    if ra == 1:
        a_spec = pl.BlockSpec((1, 1, d), lambda g, i: (g, 0, 0))
    else:
        a_spec = pl.BlockSpec((1, tm, d), lambda g, i: (g, i, 0))
    const = dict(pipeline_mode=pl.Buffered(1))
    return pl.pallas_call(
        functools.partial(_ffn_kernel, alpha=alpha),
        grid=(g_, r_ // tm),
        in_specs=[pl.BlockSpec((1, tm, d), lambda g, i: (g, i, 0)), a_spec, a_spec, a_spec,
                  pl.BlockSpec((d, d_ff), lambda g, i: (0, 0), **const),
                  pl.BlockSpec((d, d_ff), lambda g, i: (0, 0), **const),
                  pl.BlockSpec((d_ff, d), lambda g, i: (0, 0), **const),
                  pl.BlockSpec((1, d), lambda g, i: (0, 0)),
                  pl.BlockSpec((1, d), lambda g, i: (0, 0))],
        out_specs=pl.BlockSpec((1, tm, d), lambda g, i: (g, i, 0)),
        out_shape=jax.ShapeDtypeStruct(x.shape, F32),
        compiler_params=_cparams("parallel", "parallel"),
        name="ffn",
    )(x, sh, sc, gt, wg, wu, wd, lg.reshape(1, d), lb.reshape(1, d))


PROJ = {
    "fox_qp": (0, 512),
    "nsa_q": (512, 256),
    "nsa_rows": (768, 256),
    "hg_q": (1024, 256),
    "hg_f": (1280, 256),
    "hg_i": (1536, 256),
    "hg_g": (1792, 256),
    "fox_kp": (2048, 256),
    "fox_kv": (2304, 256),
    "mla_cq": (2560, 256),
    "win_rows": (2816, 128),
    "ksw": (2944, 128),
    "vsw": (3072, 128),
    "mla_ckv": (3200, 128),
    "misc": (3328, 128),
}
PROJ_W = 3456
MISC_KR, MISC_G, MISC_F = 0, 32, 44


def _proj_src_columns():
    o = dict(zip(("nsa_q", "nsa_kv", "nsa_g", "hg_q", "hg_f", "hg_i", "hg_g", "fox_q", "fox_k", "fox_v",
                  "fox_f", "mla_cq", "mla_ckv", "mla_kr"), IN_OFFS[:-1]))
    src = -np.ones((PROJ_W,), np.int64)

    def put(name, at, cols):
        base = PROJ[name][0] + at
        src[base:base + len(cols)] = cols

    hd = HEAD_DIM
    for h in range(FOX_HEADS):
        put("fox_qp", 128 * h, o["fox_q"] + hd * h + np.arange(hd))
    put("nsa_q", 0, o["nsa_q"] + np.arange(256))
    put("nsa_rows", 0, o["nsa_kv"] + np.arange(256))
    for nm in ("hg_q", "hg_f", "hg_i", "hg_g"):
        put(nm, 0, o[nm] + np.arange(256))
    for j in range(FOX_KV_HEADS):
        put("fox_kp", 128 * j, o["fox_k"] + hd * j + np.arange(hd))
    put("fox_kv", 0, o["fox_k"] + np.arange(256))
    put("mla_cq", 0, o["mla_cq"] + np.arange(MLA_D_CQ))
    put("win_rows", 0, o["nsa_kv"] + 4 * hd + np.arange(2 * hd))
    put("ksw", 0, o["nsa_kv"] + 2 * hd + np.arange(hd))
    put("ksw", hd, o["nsa_kv"] + 4 * hd + np.arange(hd))
    put("vsw", 0, o["nsa_kv"] + 3 * hd + np.arange(hd))
    put("vsw", hd, o["nsa_kv"] + 5 * hd + np.arange(hd))
    put("mla_ckv", 0, o["mla_ckv"] + np.arange(MLA_D_C))
    put("misc", MISC_KR, o["mla_kr"] + np.arange(MLA_D_ROPE))
    put("misc", MISC_G, o["nsa_g"] + np.arange(3 * NSA_HEADS))
    put("misc", MISC_F, o["fox_f"] + np.arange(FOX_HEADS))
    return src


PROJ_SRC = _proj_src_columns()


def pad_w_in(w_in_l):
    cols = jnp.asarray(np.maximum(PROJ_SRC, 0), jnp.int32)
    w = jnp.take(w_in_l, cols, axis=1)
    return jnp.where(jnp.asarray(PROJ_SRC >= 0)[None, :], w, 0.0).astype(BF16)


def pspec(name, tm):
    off, w = PROJ[name]
    return pl.BlockSpec((1, tm, w), lambda g, i, _b=off // w: (g, i, _b))


def _inproj_kernel(x_ref, sh_ref, sc_ref, w_ref, o_ref):
    h = (x_ref[0] * (1.0 + sc_ref[0]) + sh_ref[0]).astype(BF16)
    o_ref[0] = _dot(h, w_ref[...])


def inproj_call(x, sh, sc, w_pad, tm):
    g_, r_, d = x.shape
    ra = sh.shape[1]
    n = w_pad.shape[1]
    if ra == 1:
        a_spec = pl.BlockSpec((1, 1, d), lambda g, i: (g, 0, 0))
    else:
        a_spec = pl.BlockSpec((1, tm, d), lambda g, i: (g, i, 0))
    return pl.pallas_call(
        _inproj_kernel,
        grid=(g_, r_ // tm),
        in_specs=[pl.BlockSpec((1, tm, d), lambda g, i: (g, i, 0)), a_spec, a_spec,
                  pl.BlockSpec((d, n), lambda g, i: (0, 0), pipeline_mode=pl.Buffered(1))],
        out_specs=pl.BlockSpec((1, tm, n), lambda g, i: (g, i, 0)),
        out_shape=jax.ShapeDtypeStruct((g_, r_, n), F32),
        compiler_params=_cparams("parallel", "parallel"),
        name="inproj",
    )(x, sh, sc, w_pad)


MLA_DKP = 256
MLA_SCALE = (MLA_D_NOPE + MLA_D_ROPE) ** -0.5


def mla_weights(qg, kvg, wuq, wuk):
    qg_p = jnp.zeros((1, 256), F32).at[0, :MLA_D_CQ].set(qg)
    dq = MLA_D_NOPE + MLA_D_ROPE
    half = MLA_D_ROPE // 2
    cols = np.zeros((384,), np.int64)
    for h in range(MLA_HEADS):
        cols[64 * h:64 * h + 64] = dq * h + np.arange(64)
        cols[256 + half * h:256 + half * (h + 1)] = dq * h + MLA_D_NOPE + np.arange(half)
        cols[320 + half * h:320 + half * (h + 1)] = dq * h + MLA_D_NOPE + half + np.arange(half)
    wuq_p = jnp.zeros((256, 384), F32).at[:MLA_D_CQ].set(jnp.take(wuq, jnp.asarray(cols, jnp.int32), axis=1))
    wk = wuk.reshape(MLA_D_C, MLA_HEADS, MLA_D_NOPE)
    wcomb = jnp.zeros((384, MLA_HEADS * MLA_DKP), F32)
    perm = np.zeros((384, MLA_HEADS * MLA_DKP), np.float32)
    for h in range(MLA_HEADS):
        wcomb = wcomb.at[64 * h:64 * h + 64, MLA_DKP * h:MLA_DKP * h + MLA_D_C].set(wk[:, h, :].T)
        for i in range(half):
            perm[256 + half * h + i, MLA_DKP * h + MLA_D_C + i] = 1.0
            perm[320 + half * h + i, MLA_DKP * h + MLA_D_C + half + i] = 1.0
    wcomb = wcomb + jnp.asarray(perm)
    return qg_p, kvg.reshape(1, MLA_D_C), wuq_p.astype(BF16), wcomb.astype(BF16)


def rope_tables(pos):
    half = MLA_D_ROPE // 2
    inv = ROPE_BASE ** (-jnp.arange(half, dtype=F32) / half)
    ang = pos.astype(F32)[:, None] * inv[None, :]
    cos, sin = jnp.cos(ang), jnp.sin(ang)
    z = jnp.zeros((pos.shape[0], 128 - 2 * half), F32)
    zh = jnp.zeros_like(cos)
    cos4, sin4 = jnp.tile(cos, (1, MLA_HEADS)), jnp.tile(sin, (1, MLA_HEADS))
    q_c = jnp.concatenate([cos4, cos4], axis=1)
    q_s = jnp.concatenate([-sin4, sin4], axis=1)
    k_c = jnp.concatenate([cos, cos, z], axis=1)
    k_sa = jnp.concatenate([-sin, zh, z], axis=1)
    k_sb = jnp.concatenate([zh, sin, z], axis=1)
    return q_c, q_s, k_c, k_sa, k_sb


def _mla_prep_kernel(cq_ref, ckv_ref, misc_ref, qg_ref, kvg_ref, wuq_ref, wcomb_ref,
                     qc_ref, qs_ref, kc_ref, ksa_ref, ksb_ref, q_ref, rows_ref, kb_ref, vt_ref):
    cq = cq_ref[0]
    ms = jnp.sum(cq * cq, axis=-1, keepdims=True) * (1.0 / MLA_D_CQ)
    qn = ((cq * lax.rsqrt(ms + RMS_EPS)) * qg_ref[...]).astype(BF16)
    qm = _dot(qn, wuq_ref[...])
    t = qm[:, 256:384]
    r = t * qc_ref[...] + pltpu.roll(t, 64, axis=1) * qs_ref[...]
    qcat = jnp.concatenate([qm[:, :256], r], axis=1).astype(BF16)
    q_ref[0] = (_dot(qcat, wcomb_ref[...]) * MLA_SCALE).astype(BF16)
    ckv = ckv_ref[0]
    ms2 = jnp.mean(ckv * ckv, axis=-1, keepdims=True)
    ckvn = (ckv * lax.rsqrt(ms2 + RMS_EPS)) * kvg_ref[...]
    misc = misc_ref[0]
    kr = (misc * kc_ref[...] + pltpu.roll(misc, 128 - MLA_D_ROPE // 2, axis=1) * ksa_ref[...]
          + pltpu.roll(misc, MLA_D_ROPE // 2, axis=1) * ksb_ref[...])
    kfull = jnp.concatenate([ckvn, kr], axis=1)
    rows_ref[0] = kfull[:, :MLA_D_C + MLA_D_ROPE]
    if kb_ref is not None:
        kb_ref[0, 0] = kfull.astype(BF16)
        vt_ref[0, 0] = ckvn.T.astype(BF16)


def mla_prep_call(proj, wts, tabs, tm, with_kv):
    g_, r_, _ = proj.shape
    qg_p, kvg, wuq_p, wcomb = wts
    nt = r_ // tm
    if tabs[0].shape[0] == 1:
        t_spec = pl.BlockSpec((1, 128), lambda g, i: (0, 0))
    else:
        t_spec = pl.BlockSpec((tm, 128), lambda g, i: (i, 0))
    c2 = lambda g, i: (0, 0)
    out_shape = [jax.ShapeDtypeStruct((g_, r_, MLA_HEADS * MLA_DKP), BF16),
                 jax.ShapeDtypeStruct((g_, r_, MLA_D_C + MLA_D_ROPE), F32)]
    out_specs = [pl.BlockSpec((1, tm, MLA_HEADS * MLA_DKP), lambda g, i: (g, i, 0)),
                 pl.BlockSpec((1, tm, MLA_D_C + MLA_D_ROPE), lambda g, i: (g, i, 0))]
    if with_kv:
        out_shape += [jax.ShapeDtypeStruct((g_, nt, tm, MLA_DKP), BF16),
                      jax.ShapeDtypeStruct((g_, nt, MLA_D_C, tm), BF16)]
        out_specs += [pl.BlockSpec((1, 1, tm, MLA_DKP), lambda g, i: (g, i, 0, 0)),
                      pl.BlockSpec((1, 1, MLA_D_C, tm), lambda g, i: (g, i, 0, 0))]
        body = _mla_prep_kernel
    else:
        body = lambda *a: _mla_prep_kernel(*a, None, None)
    return pl.pallas_call(
        body,
        grid=(g_, nt),
        in_specs=[pspec("mla_cq", tm), pspec("mla_ckv", tm), pspec("misc", tm),
                  pl.BlockSpec((1, 256), c2), pl.BlockSpec((1, MLA_D_C), c2),
                  pl.BlockSpec((256, 384), c2), pl.BlockSpec((384, MLA_HEADS * MLA_DKP), c2),
                  t_spec, t_spec, t_spec, t_spec, t_spec],
        out_specs=out_specs,
        out_shape=out_shape,
        compiler_params=_cparams("parallel", "parallel"),
        name="mla_prep",
    )(proj, proj, proj, qg_p, kvg, wuq_p, wcomb, *tabs)


FLASH_GROUP = 4


def _flash_kernel(q_ref, k_ref, vt_ref, o_ref, m_scr, l_scr, acc_scr, *, hpg, dkp, dv, tq, tk):
    qi = pl.program_id(2)
    q = q_ref[0]
    qs = jnp.concatenate([q[:, i * dkp:(i + 1) * dkp] for i in range(hpg)], axis=0)
    r_ = hpg * tq
    m_scr[...] = jnp.full(m_scr.shape, NEG_INF, F32)
    l_scr[...] = jnp.zeros(l_scr.shape, F32)
    acc_scr[...] = jnp.zeros(acc_scr.shape, F32)

    def group(kjs, masked):
        ss = []
        for n_, kj in enumerate(kjs):
            s = _dot_nt(k_ref[0, kj], qs)
            if masked and n_ == len(kjs) - 1:
                kpos = lax.broadcasted_iota(jnp.int32, (tk, r_), 0) + (kj * tk - qi * tq)
                qpos = lax.broadcasted_iota(jnp.int32, (tk, r_), 1) & (tq - 1)
                s = jnp.where(kpos <= qpos, s, NEG_INF)
            ss.append(s)
        m_old = m_scr[...]
        m_new = m_old
        for s in ss:
            m_new = jnp.maximum(m_new, jnp.max(s, axis=0, keepdims=True))
        alpha = jnp.exp(m_old - m_new)
        l_new = alpha * l_scr[...]
        acc = alpha * acc_scr[...]
        for kj, s in zip(kjs, ss):
            p = jnp.exp(s - m_new)
            l_new = l_new + jnp.sum(p, axis=0, keepdims=True)
            acc = acc + _dot(vt_ref[0, kj], p.astype(BF16))
        l_scr[...] = l_new
        acc_scr[...] = acc
        m_scr[...] = m_new

    nfull = (qi * tq) // tk

    ngroups = nfull // FLASH_GROUP

    def body(i, c):
        group([FLASH_GROUP * i + j for j in range(FLASH_GROUP)], False)
        return c

    lax.fori_loop(0, ngroups, body, 0)
    rem = nfull - ngroups * FLASH_GROUP
    for r in range(FLASH_GROUP):
        pl.when(rem == r)(functools.partial(group, [nfull - r + j for j in range(r + 1)], True))

    o = acc_scr[...] / l_scr[...]
    for i in range(hpg):
        o_ref[0, i * dv:(i + 1) * dv, :] = o[:, i * tq:(i + 1) * tq]


def flash_call(q, k_tiles, vt_tiles, *, groups, hpg, dkp, dv, tq):
    n, s, _ = q.shape
    nk, tk = k_tiles.shape[1], k_tiles.shape[2]
    assert tq & (tq - 1) == 0 and tk % tq == 0 and s % tk == 0
    r_ = hpg * tq
    return pl.pallas_call(
        functools.partial(_flash_kernel, hpg=hpg, dkp=dkp, dv=dv, tq=tq, tk=tk),
        grid=(n, groups, s // tq),
        in_specs=[pl.BlockSpec((1, tq, hpg * dkp), lambda b, g, i: (b, i, g)),
                  pl.BlockSpec((1, nk, tk, dkp), lambda b, g, i: (b, 0, 0, g)),
                  pl.BlockSpec((1, nk, dv, tk), lambda b, g, i: (b, 0, g, 0))],
        out_specs=pl.BlockSpec((1, hpg * dv, tq), lambda b, g, i: (b, g, i)),
        out_shape=jax.ShapeDtypeStruct((n, groups * hpg * dv, s), F32),
        scratch_shapes=[pltpu.VMEM((1, r_), F32), pltpu.VMEM((1, r_), F32), pltpu.VMEM((dv, r_), F32)],
        compiler_params=_cparams("parallel", "parallel", "arbitrary"),
        name="flash",
    )(q, k_tiles, vt_tiles)


FOX_AUG0 = HEAD_DIM
FOX_SCALE = HEAD_DIM ** -0.5


def _log_sigmoid(x):
    return -(jnp.maximum(-x, 0.0) + jnp.log1p(jnp.exp(-jnp.abs(x))))


def _split3(c):
    hi = c.astype(BF16)
    r1 = c - hi.astype(F32)
    mid = r1.astype(BF16)
    lo = (r1 - mid.astype(F32)).astype(BF16)
    return hi, mid, lo


def fox_consts():
    g = FOX_HEADS // FOX_KV_HEADS
    place = np.zeros((384, 128 * (FOX_KV_HEADS + FOX_HEADS)), np.float32)
    const = np.zeros((1, 128 * (FOX_KV_HEADS + FOX_HEADS)), np.float32)
    ones0 = FOX_AUG0 + 3 * g
    for h in range(FOX_HEADS):
        j, gi = divmod(h, g)
        for p in range(3):
            place[128 * p + MISC_F + h, 128 * j + FOX_AUG0 + 3 * gi + p] = 1.0
            place[128 * p + MISC_F + h, 128 * (FOX_KV_HEADS + h) + ones0 + p] = 1.0
            const[0, 128 * (FOX_KV_HEADS + h) + FOX_AUG0 + 3 * gi + p] = -1.0
    for j in range(FOX_KV_HEADS):
        const[0, 128 * j + ones0:128 * j + ones0 + 3] = 1.0
    return jnp.asarray(place, BF16), jnp.asarray(const, F32)


def _fox_prep_kernel(qp_ref, kp_ref, kv_ref, misc_ref, bf_ref, tri_ref, place_ref, const_ref,
                     logf_ref, q_ref, k_ref, vt_ref, carry_scr):
    i = pl.program_id(1)

    @pl.when(i == 0)
    def _():
        carry_scr[...] = jnp.zeros(carry_scr.shape, F32)

    logf = _log_sigmoid(misc_ref[0] + bf_ref[...])
    logf_ref[0] = logf[:, MISC_F:MISC_F + FOX_HEADS]
    c = carry_scr[...] + _dot_hi(tri_ref[...], logf)
    tm = c.shape[0]
    carry_scr[...] = c[tm - 1:tm, :]
    hi, mid, lo = _split3(c)
    aug = _dot(jnp.concatenate([hi, mid, lo], axis=1), place_ref[...]) + const_ref[...]
    nk = 128 * FOX_KV_HEADS
    k_ref[0, 0] = (kp_ref[0] + aug[:, :nk]).astype(BF16)
    q_ref[0] = (qp_ref[0] * FOX_SCALE + aug[:, nk:]).astype(BF16)
    vt_ref[0, 0] = kv_ref[0][:, 128:].T.astype(BF16)


def fox_prep_call(proj, bf_row, tri, place, const, tm):
    g_, r_, _ = proj.shape
    nt = r_ // tm
    c2 = lambda g, i: (0, 0)
    return pl.pallas_call(
        _fox_prep_kernel,
        grid=(g_, nt),
        in_specs=[pspec("fox_qp", tm), pspec("fox_kp", tm), pspec("fox_kv", tm), pspec("misc", tm),
                  pl.BlockSpec((1, 128), c2), pl.BlockSpec((tm, tm), c2),
                  pl.BlockSpec(place.shape, c2), pl.BlockSpec(const.shape, c2)],
        out_specs=[pl.BlockSpec((1, tm, FOX_HEADS), lambda g, i: (g, i, 0)),
                   pl.BlockSpec((1, tm, 128 * FOX_HEADS), lambda g, i: (g, i, 0)),
                   pl.BlockSpec((1, 1, tm, 128 * FOX_KV_HEADS), lambda g, i: (g, i, 0, 0)),
                   pl.BlockSpec((1, 1, 128, tm), lambda g, i: (g, i, 0, 0))],
        out_shape=[jax.ShapeDtypeStruct((g_, r_, FOX_HEADS), F32),
                   jax.ShapeDtypeStruct((g_, r_, 128 * FOX_HEADS), BF16),
                   jax.ShapeDtypeStruct((g_, nt, tm, 128 * FOX_KV_HEADS), BF16),
                   jax.ShapeDtypeStruct((g_, nt, 128, tm), BF16)],
        scratch_shapes=[pltpu.VMEM((1, 128), F32)],
        compiler_params=_cparams("parallel", "arbitrary"),
        name="fox_prep",
    )(proj, proj, proj, proj, bf_row, tri, place, const)


def fox_bf_row(fox_bf_l):
    return jnp.zeros((1, 128), F32).at[0, MISC_F:MISC_F + FOX_HEADS].set(fox_bf_l)


HG_SUB = 16
HG_W = HG_HEADS * HG_DK


def hgrn_consts(t):
    r = np.arange(t)
    same = (r[:, None] // HG_SUB) == (r[None, :] // HG_SUB)
    tri = (same & (r[None, :] <= r[:, None])).astype(np.float32)
    tot = same.astype(np.float32)
    c = np.arange(HG_W)
    head = ((c[:, None] // HG_DK) == (c[None, :] // HG_DK)).astype(np.float32)
    return jnp.asarray(tri), jnp.asarray(tot), jnp.asarray(head, BF16), jnp.asarray(head, F32)


def _hgrn_gates(z, lb):
    f = lb + (1.0 - lb) * _sigmoid(z)
    return jnp.log(f), (1.0 - lb) * _sigmoid(-z)


def _hgrn_kernel(q_ref, f_ref, i_ref, lb_ref, tri_ref, tot_ref, eh_ref, mask_ref, o_ref, st_ref,
                 kk_scr, b_scr, v_scr, od_scr, *, t):
    step = pl.program_id(1)

    @pl.when(step == 0)
    def _():
        st_ref[0] = jnp.zeros(st_ref.shape[1:], F32)

    q = q_ref[0]
    v = i_ref[0]
    logf, kk = _hgrn_gates(f_ref[0], lb_ref[...])
    b = _dot_hi(tri_ref[...], logf)
    btot = _dot_hi(tot_ref[...], logf)
    pad = HG_SUB
    zpad = jnp.zeros((pad, HG_W), F32)
    kk_scr[0:pad, :] = zpad
    b_scr[0:pad, :] = zpad
    v_scr[0:pad, :] = zpad
    kk_scr[pad:pad + t, :] = kk
    b_scr[pad:pad + t, :] = b
    v_scr[pad:pad + t, :] = v
    tmod = lax.broadcasted_iota(jnp.int32, (t, HG_W), 0) & (HG_SUB - 1)
    od = jnp.zeros((t, HG_W), F32)
    for dlt in range(HG_SUB):
        ks = kk_scr[pad - dlt:pad - dlt + t, :]
        bs = b_scr[pad - dlt:pad - dlt + t, :]
        vs = v_scr[pad - dlt:pad - dlt + t, :]
        p = jnp.where(tmod >= dlt, q * ks * jnp.exp(jnp.minimum(b - bs, 0.0)), 0.0)
        od = od + _dot(p.astype(BF16), eh_ref[...]) * vs
    od_scr[...] = od
    qt = (q * jnp.exp(b)).astype(BF16)
    kh = (kk * jnp.exp(btot - b)).astype(BF16)
    dec = jnp.exp(btot)
    vb = v.astype(BF16)
    mask = mask_ref[...]
    st = st_ref[0]
    for c in range(t // HG_SUB):
        r0 = c * HG_SUB
        o_ref[0, r0:r0 + HG_SUB, :] = od_scr[r0:r0 + HG_SUB, :] + _dot_nt(qt[r0:r0 + HG_SUB], st.astype(BF16))
        upd = _dot_tn(vb[r0:r0 + HG_SUB], kh[r0:r0 + HG_SUB])
        st = dec[r0:r0 + 1, :] * st + mask * upd
    st_ref[0] = st


def hgrn_call(proj, lb_row, consts, t):
    n, s, _ = proj.shape
    tri, tot, eh, mask = consts
    c2 = lambda g, i: (0, 0)
    return pl.pallas_call(
        functools.partial(_hgrn_kernel, t=t),
        grid=(n, s // t),
        in_specs=[pspec("hg_q", t), pspec("hg_f", t), pspec("hg_i", t),
                  pl.BlockSpec((1, HG_W), c2), pl.BlockSpec((t, t), c2), pl.BlockSpec((t, t), c2),
                  pl.BlockSpec((HG_W, HG_W), c2), pl.BlockSpec((HG_W, HG_W), c2)],
        out_specs=[pl.BlockSpec((1, t, HG_W), lambda g, i: (g, i, 0)),
                   pl.BlockSpec((1, HG_W, HG_W), lambda g, i: (g, 0, 0))],
        out_shape=[jax.ShapeDtypeStruct((n, s, HG_W), F32), jax.ShapeDtypeStruct((n, HG_W, HG_W), F32)],
        scratch_shapes=[pltpu.VMEM((t + HG_SUB, HG_W), F32), pltpu.VMEM((t + HG_SUB, HG_W), F32),
                        pltpu.VMEM((t + HG_SUB, HG_W), F32), pltpu.VMEM((t, HG_W), F32)],
        compiler_params=_cparams("parallel", "arbitrary"),
        name="hgrn",
    )(proj, proj, proj, lb_row, tri, tot, eh, mask)


NSA_TQ = 128
NSA_R = NSA_HEADS * NSA_TQ
NSA_SCALE = HEAD_DIM ** -0.5


def _nsa_pool_kernel(rows_ref, pool_ref, o_ref, ot_ref):
    x = rows_ref[0][:, :2 * HEAD_DIM].astype(BF16)
    pooled = _dot(pool_ref[...], x)
    o_ref[0] = pooled
    ot_ref[0] = pooled.T


def nsa_pool_call(proj, cmp_w_l):
    n, s, _ = proj.shape
    tm = min(s, 4096)
    nb = tm // L_CMP
    w = jax.nn.softmax(cmp_w_l.astype(F32))
    t = np.arange(tm)
    onehot = jnp.asarray((t[None, :] // L_CMP) == np.arange(nb)[:, None])
    pool = jnp.where(onehot, jnp.tile(w, tm // L_CMP)[None, :], 0.0).astype(BF16)
    return pl.pallas_call(
        _nsa_pool_kernel,
        grid=(n, s // tm),
        in_specs=[pspec("nsa_rows", tm), pl.BlockSpec((nb, tm), lambda g, i: (0, 0))],
        out_specs=[pl.BlockSpec((1, nb, 128), lambda g, i: (g, i, 0)),
                   pl.BlockSpec((1, 128, nb), lambda g, i: (g, 0, i))],
        out_shape=[jax.ShapeDtypeStruct((n, s // L_CMP, 128), F32),
                   jax.ShapeDtypeStruct((n, 128, s // L_CMP), F32)],
        compiler_params=_cparams("parallel", "parallel"),
        name="nsa_pool",
    )(proj, pool)


def _nsa_prep_kernel(k_ref, v_ref, ko_ref, vo_ref, *, nt):
    for j in range(nt):
        ko_ref[0, j] = k_ref[0, j * NSA_TQ:(j + 1) * NSA_TQ, :].astype(BF16)
        vo_ref[0, j] = v_ref[0, j * NSA_TQ:(j + 1) * NSA_TQ, :].T.astype(BF16)


def nsa_prep_call(proj, tm):
    n, s, _ = proj.shape
    nt = tm // NSA_TQ
    shp = jax.ShapeDtypeStruct((n, s // NSA_TQ, NSA_TQ, 128), BF16)
    spec = pl.BlockSpec((1, nt, NSA_TQ, 128), lambda g, i: (g, i, 0, 0))
    return pl.pallas_call(
        functools.partial(_nsa_prep_kernel, nt=nt),
        grid=(n, s // tm),
        in_specs=[pspec("ksw", tm), pspec("vsw", tm)],
        out_specs=[spec, spec],
        out_shape=[shp, shp],
        compiler_params=_cparams("parallel", "parallel"),
        name="nsa_prep",
    )(proj, proj)


def _bias_c_kernel(t5_ref, o_ref):
    qi = pl.program_id(0)
    ncb = o_ref.shape[1]
    blk = lax.broadcasted_iota(jnp.int32, (ncb, NSA_TQ), 0)
    r = lax.broadcasted_iota(jnp.int32, (ncb, NSA_TQ), 1)
    dist = qi * NSA_TQ + r - (blk * L_CMP + L_CMP - 1)
    for h, bias in enumerate(_t5_bias(dist, t5_ref)):
        o_ref[0, :, h * NSA_TQ:(h + 1) * NSA_TQ] = bias


def bias_c_call(t5_table, s):
    nq, ncb = s // NSA_TQ, s // L_CMP
    return pl.pallas_call(
        _bias_c_kernel,
        grid=(nq,),
        in_specs=[pl.BlockSpec(memory_space=pltpu.SMEM)],
        out_specs=pl.BlockSpec((1, ncb, NSA_R), lambda i: (i, 0, 0)),
        out_shape=jax.ShapeDtypeStruct((nq, ncb, NSA_R), F32),
        compiler_params=_cparams("parallel"),
        name="bias_c",
    )(t5_table)


def _bias_tile_kernel(t5_ref, o_ref):
    d = pl.program_id(0)
    c = lax.broadcasted_iota(jnp.int32, (NSA_TQ, NSA_TQ), 0)
    r = lax.broadcasted_iota(jnp.int32, (NSA_TQ, NSA_TQ), 1)
    dist = d * NSA_TQ + r - c
    for h, bias in enumerate(_t5_bias(dist, t5_ref)):
        o_ref[0, :, h * NSA_TQ:(h + 1) * NSA_TQ] = bias - t5_ref[T5_BUCKETS - 1, h]


def bias_tile_call(t5_table, nd):
    return pl.pallas_call(
        _bias_tile_kernel,
        grid=(nd,),
        in_specs=[pl.BlockSpec(memory_space=pltpu.SMEM)],
        out_specs=pl.BlockSpec((1, NSA_TQ, NSA_R), lambda i: (i, 0, 0)),
        out_shape=jax.ShapeDtypeStruct((nd, NSA_TQ, NSA_R), F32),
        compiler_params=_cparams("parallel"),
        name="bias_tile",
    )(t5_table)


NSA_WIN_TILES = WINDOW // NSA_TQ + 1
NSA_FAR_GROUP = 4
NSA_NEAR_TILES = 2
assert (NSA_NEAR_TILES - 1) * NSA_TQ + 1 >= T5_THR[-1]


def _online_update_one(tile_fn, d, m_scr, l_scr, acc_scr):
    _online_update([tile_fn(d)], m_scr, l_scr, acc_scr)


def _online_update(tiles, m_scr, l_scr, acc_scr):
    ss = [s if mask is None else jnp.where(mask, s, NEG_INF) for s, mask, _ in tiles]
    m_old = m_scr[...]
    m_new = m_old
    for s in ss:
        m_new = jnp.maximum(m_new, jnp.max(s, axis=0, keepdims=True))
    alpha = jnp.exp(m_old - m_new)
    l_new = alpha * l_scr[...]
    acc = alpha * acc_scr[...]
    for s, (_, _, vt) in zip(ss, tiles):
        p = jnp.exp(s - m_new)
        l_new = l_new + jnp.sum(p, axis=0, keepdims=True)
        acc = acc + _dot(vt, p.astype(BF16))
    l_scr[...] = l_new
    acc_scr[...] = acc
    m_scr[...] = m_new


def _nsa_kernel(q_ref, misc_ref, pooled_ref, pooledt_ref, k_ref, vt_ref, bc_ref, bt_ref,
                o_ref, psum_scr, sel_scr, m_s, l_s, acc_s, m_w, l_w, acc_w, *, k_top):
    qi = pl.program_id(1)
    tq, r_ = NSA_TQ, NSA_R
    hd = HEAD_DIM
    q = q_ref[0] * NSA_SCALE
    z64 = jnp.zeros((tq, hd), F32)
    qh = [q[:, hd * h:hd * (h + 1)] for h in range(NSA_HEADS)]
    qs_sel = jnp.concatenate([jnp.concatenate([x, z64], axis=1) for x in qh], axis=0).astype(BF16)
    qs_win = jnp.concatenate([jnp.concatenate([z64, x], axis=1) for x in qh], axis=0).astype(BF16)
    lane_q = lax.broadcasted_iota(jnp.int32, (1, r_), 1) & (tq - 1)
    pos_row = qi * tq + lane_q

    ncb = pooled_ref.shape[1]
    s_c = _dot_nt(pooled_ref[0].astype(BF16), qs_sel) + bc_ref[0]
    blk_c = lax.broadcasted_iota(jnp.int32, (ncb, r_), 0)
    valid_c = blk_c * L_CMP + (L_CMP - 1) <= pos_row
    s_c = jnp.where(valid_c, s_c, NEG_INF)
    m_c = jnp.max(s_c, axis=0, keepdims=True)
    e_c = jnp.where(valid_c, jnp.exp(s_c - m_c), 0.0)
    p_c = e_c / jnp.maximum(jnp.sum(e_c, axis=0, keepdims=True), TINY)
    o_c = _dot(pooledt_ref[0][hd:2 * hd, :].astype(BF16), p_c.astype(BF16))

    psum = p_c[:, 0:tq]
    for h in range(1, NSA_HEADS):
        psum = psum + p_c[:, h * tq:(h + 1) * tq]
    psum_scr[...] = psum
    nsb = ncb // 2
    imp = psum_scr[pl.ds(0, nsb, stride=2), :] + psum_scr[pl.ds(1, nsb, stride=2), :]
    blk = lax.broadcasted_iota(jnp.int32, (nsb, tq), 0)
    cur = (qi * tq + lax.broadcasted_iota(jnp.int32, (nsb, tq), 1)) // L_SEL
    forced = (blk == 0) | (blk == cur) | (blk == cur - 1)
    score = jnp.where(forced, FORCE_SCORE, jnp.where(blk <= cur, imp, -1.0))
    sel = jnp.zeros((nsb, tq), F32)
    for _ in range(k_top):
        mx = jnp.max(score, axis=0, keepdims=True)
        first = jnp.min(jnp.where(score == mx, blk, nsb), axis=0, keepdims=True)
        pick = blk == first
        sel = jnp.where(pick, 1.0, sel)
        score = jnp.where(pick, -jnp.inf, score)
    sel_scr[...] = sel

    for ref, val in ((m_s, NEG_INF), (l_s, 0.0), (acc_s, 0.0), (m_w, NEG_INF), (l_w, 0.0), (acc_w, 0.0)):
        ref[...] = jnp.full(ref.shape, val, F32)
    row = lax.broadcasted_iota(jnp.int32, (tq, tq), 0)
    krow = lax.broadcasted_iota(jnp.int32, (tq, r_), 0)
    causal = krow <= lane_q

    def sel_mask(kj):
        r0 = sel_scr[pl.ds(2 * kj, 1), :]
        r1 = sel_scr[pl.ds(2 * kj + 1, 1), :]
        m1 = jnp.where(row < L_SEL, r0, r1) > 0.0
        return jnp.concatenate([m1] * NSA_HEADS, axis=1)

    def sel_tile(kj, bias, extra):
        s = _dot_nt(k_ref[0, kj], qs_sel)
        if bias is not None:
            s = s + bias
        mask = sel_mask(kj)
        if extra is not None:
            mask = jnp.logical_and(mask, extra)
        return s, mask, vt_ref[0, kj][0:hd, :]

    nfar = jnp.maximum(qi - 1, 0)

    def far_group(g, c):
        _online_update([sel_tile(NSA_FAR_GROUP * g + j, None, None) for j in range(NSA_FAR_GROUP)], m_s, l_s, acc_s)
        return c

    def far_single(kj, c):
        _online_update([sel_tile(kj, None, None)], m_s, l_s, acc_s)
        return c

    ngroups = nfar // NSA_FAR_GROUP
    lax.fori_loop(0, ngroups, far_group, 0)
    lax.fori_loop(ngroups * NSA_FAR_GROUP, nfar, far_single, 0)

    @pl.when(qi >= 1)
    def _():
        _online_update([sel_tile(qi - 1, bt_ref[1], None), sel_tile(qi, bt_ref[0], causal)], m_s, l_s, acc_s)

    @pl.when(qi == 0)
    def _():
        _online_update([sel_tile(qi, bt_ref[0], causal)], m_s, l_s, acc_s)

    def win_tile(d):
        kj = qi - d
        s = _dot_nt(k_ref[0, kj], qs_win)
        if d < NSA_NEAR_TILES:
            s = s + bt_ref[d]
        if d == 0:
            mask = causal
        elif d == NSA_WIN_TILES - 1:
            mask = krow >= lane_q
        else:
            mask = None
        return s, mask, vt_ref[0, kj][hd:2 * hd, :]

    @pl.when(qi >= NSA_WIN_TILES - 1)
    def _():
        _online_update([win_tile(d) for d in range(NSA_WIN_TILES - 1, -1, -1)], m_w, l_w, acc_w)

    @pl.when(qi < NSA_WIN_TILES - 1)
    def _():
        for d in range(NSA_WIN_TILES - 2, 0, -1):
            pl.when(qi >= d)(functools.partial(_online_update_one, win_tile, d, m_w, l_w, acc_w))
        _online_update([win_tile(0)], m_w, l_w, acc_w)

    gt = _sigmoid(misc_ref[0]).T
    def gate(j):
        return jnp.concatenate([gt[MISC_G + 3 * h + j:MISC_G + 3 * h + j + 1, :] for h in range(NSA_HEADS)], axis=1)
    o_s = acc_s[...] / jnp.maximum(l_s[...], TINY)
    o_w = acc_w[...] / jnp.maximum(l_w[...], TINY)
    o = gate(0) * o_c + gate(1) * o_s + gate(2) * o_w
    for h in range(NSA_HEADS):
        o_ref[0, hd * h:hd * (h + 1), :] = o[:, h * tq:(h + 1) * tq]


def nsa_call(proj, pooled, pooled_t, k_tiles, vt_tiles, bias_c, bias_t):
    n, s, _ = proj.shape
    tq = NSA_TQ
    nq, ncb = s // tq, s // L_CMP
    nsb = ncb // 2
    k_top = min(N_SEL, nsb)
    off, w = PROJ["nsa_q"]
    moff, mw = PROJ["misc"]
    return pl.pallas_call(
        functools.partial(_nsa_kernel, k_top=k_top),
        grid=(n, nq),
        in_specs=[pl.BlockSpec((1, tq, w), lambda g, i: (g, i, off // w)),
                  pl.BlockSpec((1, tq, mw), lambda g, i: (g, i, moff // mw)),
                  pl.BlockSpec((1, ncb, 128), lambda g, i: (g, 0, 0)),
                  pl.BlockSpec((1, 128, ncb), lambda g, i: (g, 0, 0)),
                  pl.BlockSpec((1, nq, tq, 128), lambda g, i: (g, 0, 0, 0)),
                  pl.BlockSpec((1, nq, 128, tq), lambda g, i: (g, 0, 0, 0)),
                  pl.BlockSpec((1, ncb, NSA_R), lambda g, i: (i, 0, 0)),
                  pl.BlockSpec((NSA_NEAR_TILES, tq, NSA_R), lambda g, i: (0, 0, 0))],
        out_specs=pl.BlockSpec((1, NSA_HEADS * HEAD_DIM, tq), lambda g, i: (g, 0, i)),
        out_shape=jax.ShapeDtypeStruct((n, NSA_HEADS * HEAD_DIM, s), F32),
        scratch_shapes=[pltpu.VMEM((ncb, tq), F32), pltpu.VMEM((nsb, tq), F32),
                        pltpu.VMEM((1, NSA_R), F32), pltpu.VMEM((1, NSA_R), F32), pltpu.VMEM((HEAD_DIM, NSA_R), F32),
                        pltpu.VMEM((1, NSA_R), F32), pltpu.VMEM((1, NSA_R), F32), pltpu.VMEM((HEAD_DIM, NSA_R), F32)],
        compiler_params=_cparams("parallel", "arbitrary"),
        name="nsa",
    )(proj, proj, pooled, pooled_t, k_tiles, vt_tiles, bias_c, bias_t)


def _group_rms(x, g):
    return (x * lax.rsqrt(jnp.mean(x * x, axis=-1, keepdims=True) + RMS_EPS)) * g


def _merge_kernel(x_ref, gt_ref, nsa_ref, hg_ref, fox_ref, lat_ref, hgg_ref, mg_ref, wuv_ref, wo_ref,
                  lg_ref, lb_ref, o_ref, *, alpha, tr):
    if tr:
        o_n = nsa_ref[0].T
        o_f = fox_ref[0].T
        o_m = _dot(wuv_ref[...], lat_ref[0].astype(BF16)).T
    else:
        o_n = nsa_ref[0]
        o_f = fox_ref[0]
        o_m = _dot(lat_ref[0].astype(BF16), wuv_ref[...])
    gw = GROUP_W
    mg = mg_ref[...]
    parts = [_group_rms(o_n, mg[:, 0:gw]),
             _group_rms(hg_ref[0], mg[:, gw:2 * gw]) * _silu(hgg_ref[0]),
             _group_rms(o_f, mg[:, 2 * gw:3 * gw]),
             _group_rms(o_m, mg[:, 3 * gw:4 * gw])]
    merged = jnp.concatenate(parts, axis=1).astype(BF16)
    y = (1.0 + gt_ref[0]) * _dot(merged, wo_ref[...])
    o_ref[0] = _post_ln(x_ref[0], y, lg_ref[...], lb_ref[...], alpha)


def merge_call(x, gt, o_nsa, o_hg, o_fox, o_lat, proj, mix_g_l, wuv, w_out, lg, lb, alpha, tm, tr):
    g_, r_, d = x.shape
    ra = gt.shape[1]
    a_spec = (pl.BlockSpec((1, 1, d), lambda g, i: (g, 0, 0)) if ra == 1
              else pl.BlockSpec((1, tm, d), lambda g, i: (g, i, 0)))
    if tr:
        f_spec = lambda w: pl.BlockSpec((1, w, tm), lambda g, i: (g, 0, i))
    else:
        f_spec = lambda w: pl.BlockSpec((1, tm, w), lambda g, i: (g, i, 0))
    c2 = lambda g, i: (0, 0)
    return pl.pallas_call(
        functools.partial(_merge_kernel, alpha=alpha, tr=tr),
        grid=(g_, r_ // tm),
        in_specs=[pl.BlockSpec((1, tm, d), lambda g, i: (g, i, 0)), a_spec,
                  f_spec(GROUP_W), pl.BlockSpec((1, tm, GROUP_W), lambda g, i: (g, i, 0)), f_spec(GROUP_W),
                  f_spec(MLA_HEADS * MLA_D_C), pspec("hg_g", tm),
                  pl.BlockSpec((1, MIX_W), c2), pl.BlockSpec(wuv.shape, c2), pl.BlockSpec((MIX_W, d), c2),
                  pl.BlockSpec((1, d), c2), pl.BlockSpec((1, d), c2)],
        out_specs=pl.BlockSpec((1, tm, d), lambda g, i: (g, i, 0)),
        out_shape=jax.ShapeDtypeStruct(x.shape, F32),
        compiler_params=_cparams("parallel", "parallel"),
        name="merge",
    )(x, gt, o_nsa, o_hg, o_fox, o_lat, proj, mix_g_l.reshape(1, MIX_W), wuv, w_out,
      lg.reshape(1, d), lb.reshape(1, d))


def mla_wuv(wuv_l, tr):
    wv = wuv_l.reshape(MLA_D_C, MLA_HEADS, MLA_D_V)
    bd = jnp.zeros((MLA_HEADS * MLA_D_C, MLA_HEADS * MLA_D_V), F32)
    for h in range(MLA_HEADS):
        bd = bd.at[MLA_D_C * h:MLA_D_C * (h + 1), MLA_D_V * h:MLA_D_V * (h + 1)].set(wv[:, h, :])
    return (bd.T if tr else bd).astype(BF16)


FOX_T = 256
MLA_T = 128
MLA_TK = 256
HG_T = 256


def prompt_mixer(proj, lw, shared):
    n, s, _ = proj.shape
    pooled, pooled_t = nsa_pool_call(proj, lw["cmp_w"])
    k_tiles, vt_tiles = nsa_prep_call(proj, min(s, 512))
    o_nsa = nsa_call(proj, pooled, pooled_t, k_tiles, vt_tiles, shared["bias_c"], shared["bias_t"])
    o_hg, st_t = hgrn_call(proj, lw["lb_row"], shared["hg_consts"], HG_T)
    logf, fq, fk, fvt = fox_prep_call(proj, lw["fox_bf_row"], shared["fox_tri"], shared["fox_place"],
                                      shared["fox_const"], FOX_T)
    o_fox = flash_call(fq, fk, fvt, groups=FOX_KV_HEADS, hpg=FOX_HEADS // FOX_KV_HEADS, dkp=128,
                       dv=HEAD_DIM, tq=FOX_T)
    mq, mrows, mk, mvt = mla_prep_call(proj, lw["mla_w"], shared["rope_p"], MLA_TK, True)
    o_lat = flash_call(mq, mk, mvt, groups=1, hpg=MLA_HEADS, dkp=MLA_DKP, dv=MLA_D_C, tq=MLA_T)
    return (o_nsa, o_hg, o_fox, o_lat), (logf, mrows, st_t)


def layer_weights(l, w_in, w_out, mix_norm_g, nsa_cmp_w, lb_all, fox_b_f, mla_q_norm_g, mla_kv_norm_g,
                  mla_w_uq, mla_w_uk, mla_w_uv, ffn_w_gate, ffn_w_up, ffn_w_down, ln_g, ln_b):
    return dict(
        w_in=pad_w_in(w_in[l]), w_out=w_out[l].astype(BF16), mix_g=mix_norm_g[l], cmp_w=nsa_cmp_w[l],
        lb_row=lb_all[l].reshape(1, HG_W).astype(F32), fox_bf_row=fox_bf_row(fox_b_f[l]),
        mla_w=mla_weights(mla_q_norm_g[l], mla_kv_norm_g[l], mla_w_uq[l], mla_w_uk[l]),
        wuv_t=mla_wuv(mla_w_uv[l], True), wuv=mla_wuv(mla_w_uv[l], False),
        wg=ffn_w_gate[l].astype(BF16), wu=ffn_w_up[l].astype(BF16), wd=ffn_w_down[l].astype(BF16),
        ln_g=ln_g[l], ln_b=ln_b[l])


def prompt_shared(t5_table, s):
    place, const = fox_consts()
    r = np.arange(FOX_T)
    return dict(
        bias_c=bias_c_call(t5_table, s), bias_t=bias_tile_call(t5_table, NSA_NEAR_TILES),
        hg_consts=hgrn_consts(HG_T), fox_tri=jnp.asarray((r[None, :] <= r[:, None]).astype(np.float32)),
        fox_place=place, fox_const=const, rope_p=rope_tables(jnp.arange(s)))


def prompt_layer(x, ada, lw, shared, alpha):
    n, s, d = x.shape
    mod = lambda j, k: ada[:, j, k][:, None, :]
    x = ffn_call(x, mod(0, 0), mod(0, 1), mod(0, 2), lw["wg"][0], lw["wu"][0], lw["wd"][0],
                 lw["ln_g"][0], lw["ln_b"][0], alpha, 512)
    proj = inproj_call(x, mod(1, 0), mod(1, 1), lw["w_in"], 512)
    (o_nsa, o_hg, o_fox, o_lat), (logf, mrows, st_t) = prompt_mixer(proj, lw, shared)
    x = merge_call(x, mod(1, 2), o_nsa, o_hg, o_fox, o_lat, proj, lw["mix_g"], lw["wuv_t"], lw["w_out"],
                   lw["ln_g"][1], lw["ln_b"][1], alpha, 512, True)
    x = ffn_call(x, mod(2, 0), mod(2, 1), mod(2, 2), lw["wg"][1], lw["wu"][1], lw["wd"][1],
                 lw["ln_g"][2], lw["ln_b"][2], alpha, 512)
    o, w = PROJ["nsa_rows"]
    nsa_rows = proj[:, :, o:o + w].reshape(n, s, 4, HEAD_DIM)
    o, w = PROJ["win_rows"]
    new_win = proj[:, s - min(WINDOW, s):, o:o + w].reshape(n, min(WINDOW, s), 2, HEAD_DIM)
    o, w = PROJ["fox_kv"]
    fox_rows = proj[:, :, o:o + w].reshape(n, s, 2, FOX_KV_HEADS, HEAD_DIM)
    return x, (nsa_rows, new_win, fox_rows, logf, mrows, hgrn_state_from_t(st_t))


DEC_PP = 32
QROWS = 8


def _page_specs(pp, block, row_blk, base):
    return [pl.BlockSpec(block, lambda n, j, pt, _i=i: (base + pt[n, j * pp + _i], row_blk, 0)) for i in range(pp)]


def _paged_attn_kernel(pt_ref, q_ref, new_ref, *rest, pp, rk, v0, has_bias):
    if has_bias:
        d_ref, rest = rest[0], rest[1:]
    pages, o_ref, m_scr, l_scr, acc_scr = rest[:pp], rest[pp], rest[pp + 1], rest[pp + 2], rest[pp + 3]
    j = pl.program_id(1)
    dv = acc_scr.shape[1]

    @pl.when(j == 0)
    def _():
        m_scr[...] = jnp.full(m_scr.shape, NEG_INF, F32)
        l_scr[...] = jnp.zeros(l_scr.shape, F32)
        acc_scr[...] = jnp.zeros(acc_scr.shape, F32)

    q = q_ref[0]
    kt = jnp.concatenate([p[0, 0:rk, :].astype(BF16) for p in pages], axis=1)
    vt = jnp.concatenate([p[0, v0:v0 + dv, :].astype(BF16) for p in pages], axis=1)
    s = _dot(q, kt)
    if has_bias:
        s = s + jnp.concatenate([d_ref[0, :, i, :] for i in range(pp)], axis=1)
    m_old = m_scr[...]
    m_new = jnp.maximum(m_old, jnp.max(s, axis=1, keepdims=True))
    alpha = jnp.exp(m_old - m_new)
    p = jnp.exp(s - m_new)
    l_scr[...] = alpha * l_scr[...] + jnp.sum(p, axis=1, keepdims=True)
    acc_scr[...] = alpha * acc_scr[...] + _dot_nt(p.astype(BF16), vt)
    m_scr[...] = m_new

    @pl.when(j == pl.num_programs(1) - 1)
    def _():
        new = new_ref[0].astype(BF16).astype(F32)
        s_new = jnp.sum(q.astype(F32) * new[:, 0:rk], axis=1, keepdims=True)
        m_old = m_scr[...]
        m_f = jnp.maximum(m_old, s_new)
        a = jnp.exp(m_old - m_f)
        p_new = jnp.exp(s_new - m_f)
        l_f = a * l_scr[...] + p_new
        o_ref[0] = (a * acc_scr[...] + p_new.astype(BF16).astype(F32) * new[:, v0:v0 + dv]) / l_f


def paged_attn_call(page_table, pool_t, base, q8, new_rows, bias, rk, v0, dv):
    n, npages = page_table.shape
    r = pool_t.shape[1]
    pp = min(DEC_PP, npages)
    in_specs = [pl.BlockSpec((1, QROWS, rk), lambda b, j, pt: (b, 0, 0)),
                pl.BlockSpec((1, 1, r), lambda b, j, pt: (b, 0, 0))]
    args = [q8, new_rows]
    if bias is not None:
        in_specs.append(pl.BlockSpec((1, QROWS, pp, PAGE_SIZE), lambda b, j, pt: (b, 0, j, 0)))
        args.append(bias)
    in_specs += _page_specs(pp, (1, r, PAGE_SIZE), 0, base)
    args += [pool_t] * pp
    return pl.pallas_call(
        functools.partial(_paged_attn_kernel, pp=pp, rk=rk, v0=v0, has_bias=bias is not None),
        grid_spec=pltpu.PrefetchScalarGridSpec(
            num_scalar_prefetch=1, grid=(n, npages // pp), in_specs=in_specs,
            out_specs=pl.BlockSpec((1, QROWS, dv), lambda b, j, pt: (b, 0, 0)),
            scratch_shapes=[pltpu.VMEM((QROWS, 1), F32), pltpu.VMEM((QROWS, 1), F32), pltpu.VMEM((QROWS, dv), F32)]),
        out_shape=jax.ShapeDtypeStruct((n, QROWS, dv), F32),
        compiler_params=_cparams("parallel", "arbitrary"),
        name="paged_attn",
    )(page_table, *args)


def _fox_dec_bias_kernel(lg_ref, misc_ref, bf_ref, after_ref, later_ref, d_ref, logf_ref):
    logf_new = _log_sigmoid(misc_ref[0] + bf_ref[...])
    logf_ref[0] = logf_new
    ones = jnp.ones((PAGE_SIZE, PAGE_SIZE), F32)
    for h in range(FOX_HEADS):
        x = lg_ref[0, h]
        within = _dot_hi(x, after_ref[...])
        across = _dot_hi(later_ref[...], _dot_hi(x, ones))
        d_ref[0, h] = within + across + logf_new[:, MISC_F + h:MISC_F + h + 1]
    for h in range(FOX_HEADS, QROWS):
        d_ref[0, h] = jnp.zeros(d_ref.shape[2:], F32)


def fox_dec_bias_call(logf_pages, misc_rows, bf_row):
    n, _, npages, _ = logf_pages.shape
    pos = np.arange(PAGE_SIZE)
    after = jnp.asarray((pos[:, None] > pos[None, :]).astype(np.float32))
    pg = np.arange(npages)
    later = jnp.asarray((pg[None, :] > pg[:, None]).astype(np.float32))
    return pl.pallas_call(
        _fox_dec_bias_kernel,
        grid=(n,),
        in_specs=[pl.BlockSpec((1, FOX_HEADS, npages, PAGE_SIZE), lambda b: (b, 0, 0, 0)),
                  pl.BlockSpec((1, 1, 128), lambda b: (b, 0, 0)),
                  pl.BlockSpec((1, 128), lambda b: (0, 0)),
                  pl.BlockSpec((PAGE_SIZE, PAGE_SIZE), lambda b: (0, 0)),
                  pl.BlockSpec((npages, npages), lambda b: (0, 0))],
        out_specs=[pl.BlockSpec((1, QROWS, npages, PAGE_SIZE), lambda b: (b, 0, 0, 0)),
                   pl.BlockSpec((1, 1, 128), lambda b: (b, 0, 0))],
        out_shape=[jax.ShapeDtypeStruct((n, QROWS, npages, PAGE_SIZE), F32),
                   jax.ShapeDtypeStruct((n, 1, 128), F32)],
        compiler_params=_cparams("parallel"),
        name="fox_dec_bias",
    )(logf_pages, misc_rows, bf_row, after, later)


NSA_POOL_PP = 32


def _nsa_pool_dec_kernel(pt_ref, pool_ref, *rest, pp):
    pages, o_ref = rest[:pp], rest[pp]
    x = jnp.concatenate([p[0].astype(BF16) for p in pages], axis=1)
    o_ref[0] = _dot(x, pool_ref[...])


def nsa_pool_dec_call(page_table, pool_t, base, cmp_w_l):
    n, npages = page_table.shape
    pp = min(NSA_POOL_PP, npages)
    tm = pp * PAGE_SIZE
    nb = tm // L_CMP
    w = jax.nn.softmax(cmp_w_l.astype(F32))
    t = np.arange(tm)
    onehot = jnp.asarray((t[:, None] // L_CMP) == np.arange(nb)[None, :])
    pmat = jnp.where(onehot, jnp.tile(w, tm // L_CMP)[:, None], 0.0).astype(BF16)
    return pl.pallas_call(
        functools.partial(_nsa_pool_dec_kernel, pp=pp),
        grid_spec=pltpu.PrefetchScalarGridSpec(
            num_scalar_prefetch=1, grid=(n, npages // pp),
            in_specs=[pl.BlockSpec((tm, nb), lambda b, j, pt: (0, 0))] + _page_specs(pp, (1, 128, PAGE_SIZE), 0, base),
            out_specs=pl.BlockSpec((1, 128, nb), lambda b, j, pt: (b, 0, j))),
        out_shape=jax.ShapeDtypeStruct((n, 128, npages * PAGE_SIZE // L_CMP), F32),
        compiler_params=_cparams("parallel", "parallel"),
        name="nsa_pool_dec",
    )(page_table, pmat, *([pool_t] * pp))


assert T5_THR[-1] <= PAGE_SIZE


def _bias_rows_kernel(t5_ref, oc_ref, ow_ref, od_ref, *, t_len, wb):
    ncb = oc_ref.shape[1]
    dist_c = t_len - (lax.broadcasted_iota(jnp.int32, (1, ncb), 1) * L_CMP + L_CMP - 1)
    dist_w = wb - lax.broadcasted_iota(jnp.int32, (1, wb), 1)
    lane = lax.broadcasted_iota(jnp.int32, (1, 2 * PAGE_SIZE), 1)
    dist_d = jnp.where(lane < PAGE_SIZE, PAGE_SIZE - lane, lane - PAGE_SIZE)
    for ref, dist in ((oc_ref, dist_c), (ow_ref, dist_w), (od_ref, dist_d)):
        ref[...] = jnp.zeros(ref.shape, F32)
        for h, bias in enumerate(_t5_bias(dist, t5_ref)):
            ref[h:h + 1, :] = bias


def bias_rows_call(t5_table, t_len, wb):
    ncb = t_len // L_CMP
    return pl.pallas_call(
        functools.partial(_bias_rows_kernel, t_len=t_len, wb=wb),
        in_specs=[pl.BlockSpec(memory_space=pltpu.SMEM)],
        out_shape=[jax.ShapeDtypeStruct((QROWS, ncb), F32), jax.ShapeDtypeStruct((QROWS, wb), F32),
                   jax.ShapeDtypeStruct((QROWS, 2 * PAGE_SIZE), F32)],
        name="bias_rows",
    )(t5_table)


def _nsa_cmp_dec_kernel(q_ref, pooled_ref, bc_ref, pair_ref, oc_ref, imp_ref):
    kvt = pooled_ref[0].astype(BF16)
    s = _dot(q_ref[0], kvt[0:HEAD_DIM, :]) + bc_ref[...]
    m = jnp.max(s, axis=1, keepdims=True)
    e = jnp.exp(s - m)
    p = e / jnp.maximum(jnp.sum(e, axis=1, keepdims=True), TINY)
    oc_ref[0] = _dot_nt(p.astype(BF16), kvt[HEAD_DIM:2 * HEAD_DIM, :])
    head = lax.broadcasted_iota(jnp.int32, p.shape, 0) < NSA_HEADS
    psum = jnp.sum(jnp.where(head, p, 0.0), axis=0, keepdims=True)
    imp_ref[0] = _dot_hi(jnp.broadcast_to(psum, (QROWS, psum.shape[1])), pair_ref[...])[0:1, :]


def nsa_cmp_dec_call(q8, pooled_t, bias_c):
    n, _, ncb = pooled_t.shape
    b = np.arange(ncb)
    pair = jnp.asarray((b[:, None] // 2 == np.arange(ncb // 2)[None, :]).astype(np.float32))
    return pl.pallas_call(
        _nsa_cmp_dec_kernel,
        grid=(n,),
        in_specs=[pl.BlockSpec((1, QROWS, HEAD_DIM), lambda i: (i, 0, 0)),
                  pl.BlockSpec((1, 128, ncb), lambda i: (i, 0, 0)),
                  pl.BlockSpec((QROWS, ncb), lambda i: (0, 0)),
                  pl.BlockSpec((ncb, ncb // 2), lambda i: (0, 0))],
        out_specs=[pl.BlockSpec((1, QROWS, HEAD_DIM), lambda i: (i, 0, 0)),
                   pl.BlockSpec((1, 1, ncb // 2), lambda i: (i, 0, 0))],
        out_shape=[jax.ShapeDtypeStruct((n, QROWS, HEAD_DIM), F32), jax.ShapeDtypeStruct((n, 1, ncb // 2), F32)],
        compiler_params=_cparams("parallel"),
        name="nsa_cmp_dec",
    )(q8, pooled_t, bias_c, pair)


def _topk_dec_kernel(imp_ref, idx_ref, *, t_len, k_top):
    imp_t = imp_ref[...].T
    nfull, n = imp_t.shape
    rows = idx_ref.shape[0]
    cur = t_len // L_SEL
    score = jnp.concatenate([imp_t, jnp.zeros((SUBLANES, n), F32)], axis=0)
    blk = lax.broadcasted_iota(jnp.int32, score.shape, 0)
    forced = (blk == 0) | (blk == cur) | (blk == cur - 1)
    score = jnp.where(forced, FORCE_SCORE, jnp.where(blk <= cur, score, -jnp.inf))
    big = nfull + SUBLANES
    picks = []
    for _ in range(k_top):
        mx = jnp.max(score, axis=0, keepdims=True)
        first = jnp.min(jnp.where(score == mx, blk, big), axis=0, keepdims=True)
        picks.append(first)
        score = jnp.where(blk == first, -jnp.inf, score)
    picks += [jnp.zeros((1, n), jnp.int32)] * (rows - k_top)
    idx_ref[...] = jnp.concatenate(picks, axis=0)


def topk_dec_call(imp, t_len):
    n = imp.shape[0]
    nsb = -(-(t_len + 1) // L_SEL)
    k_top = min(N_SEL, nsb)
    return pl.pallas_call(
        functools.partial(_topk_dec_kernel, t_len=t_len, k_top=k_top),
        out_shape=jax.ShapeDtypeStruct((N_SEL, n), jnp.int32),
        name="topk_dec",
    )(imp)


def _nsa_selwin_dec_kernel(pt_ref, idx_ref, q_ref, g_ref, oc_ref, snew_ref, wnew_ref, win_ref, bw_ref, bd_ref,
                           *rest, t_len, k_top):
    pages, o_ref = rest[:N_SEL], rest[N_SEL]
    b = pl.program_id(0)
    hd = HEAD_DIM
    q = q_ref[0]
    qf = q.astype(F32)
    cur = t_len // L_SEL
    last_page = t_len // PAGE_SIZE - 1
    bd = bd_ref[...]
    near, far, bias0 = bd[:, 0:PAGE_SIZE], bd[:, 0:1], bd[:, PAGE_SIZE:PAGE_SIZE + 1]
    kt = jnp.concatenate([pg[0, 0:hd, :].astype(BF16) for pg in pages], axis=1)
    vt = jnp.concatenate([pg[0, hd:2 * hd, :].astype(BF16) for pg in pages], axis=1)
    nk = N_SEL * PAGE_SIZE
    lane = lax.broadcasted_iota(jnp.int32, (1, nk), 1)
    slot = lane // PAGE_SIZE
    in_page = lane & (PAGE_SIZE - 1)
    blk = jnp.full((1, nk), -1, jnp.int32)
    biases = []
    far_tile = jnp.broadcast_to(far, (QROWS, PAGE_SIZE))
    for i in range(N_SEL):
        if i < k_top:
            bi = idx_ref[b * N_SEL + i]
            blk = jnp.where(slot == i, bi, blk)
            biases.append(jnp.where(bi // 2 == last_page, near, far_tile))
        else:
            biases.append(far_tile)
    bias = jnp.concatenate(biases, axis=1)
    valid = (blk >= 0) & (blk < cur) & ((in_page // L_SEL) == (blk & 1))
    s = jnp.where(valid, _dot(q, kt) + bias, NEG_INF)
    snew = snew_ref[0].astype(BF16).astype(F32)
    s_new = jnp.sum(qf * snew[:, 0:hd], axis=1, keepdims=True) + bias0
    m = jnp.maximum(jnp.max(s, axis=1, keepdims=True), s_new)
    e = jnp.where(valid, jnp.exp(s - m), 0.0)
    e_new = jnp.exp(s_new - m)
    den = jnp.maximum(jnp.sum(e, axis=1, keepdims=True) + e_new, TINY)
    o_s = (_dot_nt(e.astype(BF16), vt) + e_new.astype(BF16).astype(F32) * snew[:, hd:2 * hd]) / den
    winb = win_ref[0].astype(BF16)
    wnb = wnew_ref[0].astype(BF16).astype(F32)
    s_w = _dot(q, winb[0:hd, :]) + bw_ref[...]
    sw_new = jnp.sum(qf * wnb[:, 0:hd], axis=1, keepdims=True) + bias0
    m_w = jnp.maximum(jnp.max(s_w, axis=1, keepdims=True), sw_new)
    e_w = jnp.exp(s_w - m_w)
    ew_new = jnp.exp(sw_new - m_w)
    den_w = jnp.maximum(jnp.sum(e_w, axis=1, keepdims=True) + ew_new, TINY)
    o_w = (_dot_nt(e_w.astype(BF16), winb[hd:2 * hd, :]) + ew_new.astype(BF16).astype(F32) * wnb[:, hd:2 * hd]) / den_w
    g = _sigmoid(g_ref[0])
    o_ref[0] = g[:, 0:1] * oc_ref[0] + g[:, 1:2] * o_s + g[:, 2:3] * o_w


def nsa_selwin_dec_call(page_table, idx, pool_t, base, q8, g8, o_c, sel_new, win_new, win_t, wbase, bias_w, bias_d,
                        t_len):
    n, npages = page_table.shape
    wb = win_t.shape[2]
    nsb = -(-(t_len + 1) // L_SEL)
    k_top = min(N_SEL, nsb)
    row = lambda w: pl.BlockSpec((1, 1, w), lambda b, pt, ix: (b, 0, 0))
    q_spec = pl.BlockSpec((1, QROWS, HEAD_DIM), lambda b, pt, ix: (b, 0, 0))
    pg_specs = [pl.BlockSpec((1, 128, PAGE_SIZE),
                             lambda b, pt, ix, _i=i: (base + pt[b, jnp.clip(ix[b * N_SEL + _i] // 2, 0, npages - 1)], 1, 0))
                for i in range(N_SEL)]
    return pl.pallas_call(
        functools.partial(_nsa_selwin_dec_kernel, t_len=t_len, k_top=k_top),
        grid_spec=pltpu.PrefetchScalarGridSpec(
            num_scalar_prefetch=2, grid=(n,),
            in_specs=[q_spec, pl.BlockSpec((1, QROWS, 128), lambda b, pt, ix: (b, 0, 0)), q_spec, row(128), row(128),
                      pl.BlockSpec((1, 128, wb), lambda b, pt, ix: (wbase + b, 0, 0)),
                      pl.BlockSpec((QROWS, wb), lambda b, pt, ix: (0, 0)),
                      pl.BlockSpec((QROWS, 2 * PAGE_SIZE), lambda b, pt, ix: (0, 0))] + pg_specs,
            out_specs=q_spec),
        out_shape=jax.ShapeDtypeStruct((n, QROWS, HEAD_DIM), F32),
        compiler_params=_cparams("parallel"),
        name="nsa_selwin_dec",
    )(page_table, idx, q8, g8, o_c, sel_new, win_new, win_t, bias_w, bias_d, *([pool_t] * N_SEL))


def _hgrn_dec_kernel(q_ref, z_ref, v_ref, lb_ref, s_ref, o_ref, so_ref):
    logf, kk = _hgrn_gates(z_ref[0], lb_ref[0])
    f = jnp.exp(logf)
    q = q_ref[0]
    v = v_ref[0]
    acc = jnp.zeros(v.shape, F32)
    for k in range(HG_DK):
        s1 = f[k:k + 1, :] * s_ref[0, k] + kk[k:k + 1, :] * v
        so_ref[0, k] = s1
        acc = acc + q[k:k + 1, :] * s1
    o_ref[0] = acc


def hgrn_dec_call(q_t, z_t, v_t, lb_t, state_t, hbase):
    h = q_t.shape[0]
    _, dk, dv, n = state_t.shape
    r3 = lambda a: pl.BlockSpec((1, a, n), lambda i: (i, 0, 0))
    s_spec = pl.BlockSpec((1, dk, dv, n), lambda i: (i, 0, 0, 0))
    return pl.pallas_call(
        _hgrn_dec_kernel,
        grid=(h,),
        in_specs=[r3(dk), r3(dk), r3(dv), r3(dk), pl.BlockSpec((1, dk, dv, n), lambda i: (hbase + i, 0, 0, 0))],
        out_specs=[r3(dv), s_spec],
        out_shape=[jax.ShapeDtypeStruct((h, dv, n), F32), jax.ShapeDtypeStruct((h, dk, dv, n), F32)],
        compiler_params=_cparams("parallel"),
        name="hgrn_dec",
    )(q_t, z_t, v_t, lb_t, state_t)


def _pad_rows(x, rows=QROWS):
    return jnp.pad(x, ((0, 0), (0, rows - x.shape[1]), (0, 0)))


def decode_caches(cache_nsa_kv, cache_nsa_win, cache_fox_kv, cache_fox_logf, cache_mla, state_hgrn):
    depth, n_phys = cache_nsa_kv.shape[:2]
    n, wb = cache_nsa_win.shape[1:3]
    return dict(
        n_phys=n_phys,
        nsa_t=jnp.transpose(cache_nsa_kv, (0, 1, 3, 4, 2)).reshape(depth * n_phys, 4 * HEAD_DIM, PAGE_SIZE),
        win_t=jnp.transpose(cache_nsa_win, (0, 1, 3, 4, 2)).reshape(depth * n, 2 * HEAD_DIM, wb),
        fox_t=jnp.transpose(cache_fox_kv, (0, 1, 3, 4, 5, 2)).reshape(depth * n_phys, 4 * HEAD_DIM, PAGE_SIZE),
        logf_t=jnp.swapaxes(cache_fox_logf, 2, 3).reshape(depth * n_phys, FOX_HEADS, PAGE_SIZE),
        mla_t=jnp.swapaxes(cache_mla, 2, 3).reshape(depth * n_phys, MLA_D_C + MLA_D_ROPE, PAGE_SIZE),
        hg_t=jnp.transpose(state_hgrn.astype(F32), (0, 2, 3, 4, 1)).reshape(depth * HG_HEADS, HG_DK, HG_DV, n),
        win=cache_nsa_win)


def decode_layer(x, ada, lw, shared, caches, l, page_table, alpha):
    _, n, d = x.shape
    n_phys = caches["n_phys"]
    base = l * n_phys
    npages = page_table.shape[1]
    t_len = npages * PAGE_SIZE
    mod = lambda j, k: ada[:, j, k][None]
    x = ffn_call(x, mod(0, 0), mod(0, 1), mod(0, 2), lw["wg"][0], lw["wu"][0], lw["wd"][0],
                 lw["ln_g"][0], lw["ln_b"][0], alpha, n)
    proj = inproj_call(x, mod(1, 0), mod(1, 1), lw["w_in"], n)
    p2 = proj[0]
    col = lambda name: p2[:, PROJ[name][0]:PROJ[name][0] + PROJ[name][1]]
    misc = col("misc")
    mq, mrows = mla_prep_call(proj, lw["mla_w"], shared["rope_d"], n, False)
    dk_m = MLA_D_C + MLA_D_ROPE
    q8 = _pad_rows(mq[0].reshape(n, MLA_HEADS, MLA_DKP)[:, :, :dk_m])
    o_lat = paged_attn_call(page_table, caches["mla_t"], base, q8, mrows[0][:, None, :], None, dk_m, 0, MLA_D_C)
    o_lat = o_lat[:, :MLA_HEADS].reshape(1, n, MLA_HEADS * MLA_D_C)
    g = FOX_HEADS // FOX_KV_HEADS
    fq = col("fox_qp").reshape(n, FOX_HEADS, 128)[:, :, :HEAD_DIM] * FOX_SCALE
    q_bd = jnp.zeros((n, FOX_HEADS, 2 * HEAD_DIM), F32)
    for h in range(FOX_HEADS):
        q_bd = q_bd.at[:, h, HEAD_DIM * (h // g):HEAD_DIM * (h // g + 1)].set(fq[:, h])
    logf_pages = jnp.swapaxes(caches["logf_t"][page_table + base], 1, 2)
    fbias, logf_row = fox_dec_bias_call(logf_pages, misc[:, None, :], lw["fox_bf_row"])
    o_fox = paged_attn_call(page_table, caches["fox_t"], base, _pad_rows(q_bd).astype(BF16),
                            col("fox_kv")[:, None, :], fbias, 2 * HEAD_DIM, 2 * HEAD_DIM, 2 * HEAD_DIM)
    o_fox = jnp.concatenate([o_fox[:, h, HEAD_DIM * (h // g):HEAD_DIM * (h // g + 1)]
                             for h in range(FOX_HEADS)], axis=-1)[None]
    nsa_t = caches["nsa_t"]
    pooled_t = nsa_pool_dec_call(page_table, nsa_t, base, lw["cmp_w"])
    qn8 = _pad_rows(col("nsa_q").reshape(n, NSA_HEADS, HEAD_DIM) * NSA_SCALE).astype(BF16)
    o_c, imp = nsa_cmp_dec_call(qn8, pooled_t, shared["bias_c_d"])
    idx = topk_dec_call(imp[:, 0, :], t_len).T.reshape(-1)
    g8 = _pad_rows(jnp.pad(misc[:, MISC_G:MISC_G + 3 * NSA_HEADS].reshape(n, NSA_HEADS, 3),
                           ((0, 0), (0, 0), (0, 125))))
    win_rows = col("win_rows")
    o_nsa = nsa_selwin_dec_call(page_table, idx, nsa_t, base, qn8, g8, o_c, col("nsa_rows")[:, None, 2 * HEAD_DIM:],
                                win_rows[:, None, :], caches["win_t"], l * n, shared["bias_w_d"], shared["bias_d_d"],
                                t_len)
    o_nsa = o_nsa[:, :NSA_HEADS].reshape(1, n, GROUP_W)
    win_out = jnp.concatenate([caches["win"][l, :, 1:], win_rows.reshape(n, 1, 2, HEAD_DIM)], axis=1)
    tr = lambda name: col(name).T.reshape(HG_HEADS, HG_DK, n)
    lb_t = jnp.broadcast_to(lw["lb_row"].reshape(HG_HEADS, HG_DK, 1), (HG_HEADS, HG_DK, n))
    o_hg, hg_new = hgrn_dec_call(tr("hg_q"), tr("hg_f"), tr("hg_i"), lb_t, caches["hg_t"], l * HG_HEADS)
    o_hg = o_hg.reshape(GROUP_W, n).T[None]
    x = merge_call(x, mod(1, 2), o_nsa, o_hg, o_fox, o_lat, proj, lw["mix_g"], lw["wuv"], lw["w_out"],
                   lw["ln_g"][1], lw["ln_b"][1], alpha, n, False)
    x = ffn_call(x, mod(2, 0), mod(2, 1), mod(2, 2), lw["wg"][1], lw["wu"][1], lw["wd"][1],
                 lw["ln_g"][2], lw["ln_b"][2], alpha, n)
    st = (col("nsa_rows").reshape(n, 1, 4, HEAD_DIM), win_out,
          col("fox_kv").reshape(n, 1, 2, FOX_KV_HEADS, HEAD_DIM),
          logf_row[:, :, MISC_F:MISC_F + FOX_HEADS], mrows[0][:, None, :],
          jnp.transpose(hg_new, (3, 0, 1, 2)))
    return x, st


def kernel(x_prompt, x_sample, c_prompt, c_sample, page_table, cache_nsa_kv, cache_nsa_win, cache_fox_kv,
           cache_fox_logf, cache_mla, state_hgrn, w_in, w_out, mix_norm_g, nsa_cmp_w, t5_table, hgrn_lb_logits,
           fox_b_f, mla_q_norm_g, mla_kv_norm_g, mla_w_uq, mla_w_uk, mla_w_uv, ffn_w_gate, ffn_w_up, ffn_w_down,
           ada_w, ada_b, ln_g, ln_b):
    depth = w_in.shape[0]
    alpha = (2 * depth) ** 0.25
    nb, s, d = x_prompt.shape
    nd = x_sample.shape[0]
    assert x_sample.shape[1] == 1
    t_len = page_table.shape[1] * PAGE_SIZE
    lb_p = jax.nn.softmax(hgrn_lb_logits.astype(F32), axis=0)
    lb_all = jnp.cumsum(lb_p, axis=0) - lb_p
    shared = prompt_shared(t5_table, s)
    bias_c_d, bias_w_d, bias_d_d = bias_rows_call(t5_table, t_len, cache_nsa_win.shape[2])
    shared.update(bias_c_d=bias_c_d, bias_w_d=bias_w_d, bias_d_d=bias_d_d,
                  rope_d=rope_tables(jnp.full((1,), t_len, jnp.int32)))
    c_all = jnp.concatenate([c_prompt, c_sample], axis=0)
    c_all = jnp.pad(c_all, ((0, -c_all.shape[0] % SUBLANES), (0, 0)))
    xp, xs = x_prompt, x_sample.reshape(1, nd, d)
    caches = decode_caches(cache_nsa_kv, cache_nsa_win, cache_fox_kv, cache_fox_logf, cache_mla, state_hgrn)
    st_p, st_s = [], []
    for l in range(depth):
        lw = layer_weights(l, w_in, w_out, mix_norm_g, nsa_cmp_w, lb_all, fox_b_f, mla_q_norm_g, mla_kv_norm_g,
                           mla_w_uq, mla_w_uk, mla_w_uv, ffn_w_gate, ffn_w_up, ffn_w_down, ln_g, ln_b)
        ada = ada_call(c_all, ada_w[l], ada_b[l]).reshape(-1, N_SUB, 3, d)
        xp, sp = prompt_layer(xp, ada[:nb], lw, shared, alpha)
        xs, ss = decode_layer(xs, ada[nb:nb + nd], lw, shared, caches, l, page_table, alpha)
        st_p.append(sp)
        st_s.append(ss)
    outs = [xp, xs.reshape(nd, 1, d)]
    for group in (st_p, st_s):
        outs += [jnp.stack([t[i] for t in group]) for i in range(6)]
    return tuple(outs)


def hgrn_state_from_t(st_t):
    n = st_t.shape[0]
    blk = st_t.reshape(n, HG_HEADS, HG_DV, HG_HEADS, HG_DK)
    diag = jnp.stack([blk[:, h, :, h, :] for h in range(HG_HEADS)], axis=1)
    return jnp.swapaxes(diag, -1, -2)
```

```python
import functools
import math

import jax
import jax.numpy as jnp
import numpy as np
from jax import lax
from jax.experimental import pallas as pl
from jax.experimental.pallas import tpu as pltpu

F32 = jnp.float32
BF16 = jnp.bfloat16

HEAD_DIM = 64
N_GROUPS = 4
GROUP_W = 256
MIX_W = N_GROUPS * GROUP_W
NSA_HEADS = 4
L_CMP = 32
L_SEL = 64
N_SEL = 16
WINDOW = 512
FORCE_SCORE = 1.0e4
HG_HEADS = 4
HG_DK = 64
HG_DV = 64
FOX_HEADS = 4
FOX_KV_HEADS = 2
MLA_HEADS = 4
MLA_D_CQ = 192
MLA_D_C = 128
MLA_D_NOPE = 64
MLA_D_ROPE = 32
MLA_D_V = 64
ROPE_BASE = 10000.0
T5_BUCKETS = 32
T5_MAX_EXACT = 16
T5_MAX_DIST = 128
N_SUB = 3
LN_EPS = 1e-5
RMS_EPS = 1e-6
NEG_INF = -1e30
TINY = 1e-30
PAGE_SIZE = 128

LANES = 128
SUBLANES = 8
VMEM_LIMIT = 56 * 1024 * 1024

IN_SIZES = (NSA_HEADS * HEAD_DIM, 6 * HEAD_DIM, 3 * NSA_HEADS,
            HG_HEADS * HG_DK, HG_HEADS * HG_DK, HG_HEADS * HG_DV, HG_HEADS * HG_DV,
            FOX_HEADS * HEAD_DIM, FOX_KV_HEADS * HEAD_DIM, FOX_KV_HEADS * HEAD_DIM, FOX_HEADS,
            MLA_D_CQ, MLA_D_C, MLA_D_ROPE)
IN_OFFS = tuple(int(v) for v in np.cumsum((0,) + IN_SIZES))


def _t5_thresholds():
    d = np.arange(0, 4 * T5_MAX_DIST)
    nf = np.maximum(d, 1).astype(np.float64)
    large = T5_MAX_EXACT + (np.log(nf / T5_MAX_EXACT) / math.log(T5_MAX_DIST / T5_MAX_EXACT)
                            * (T5_BUCKETS - T5_MAX_EXACT)).astype(np.int64)
    bucket = np.where(d < T5_MAX_EXACT, d, np.minimum(large, T5_BUCKETS - 1))
    return tuple(int(np.argmax(bucket >= b)) for b in range(1, T5_BUCKETS))


T5_THR = _t5_thresholds()


def _cparams(*sem):
    return pltpu.CompilerParams(dimension_semantics=sem, vmem_limit_bytes=VMEM_LIMIT)


def _dot(a, b):
    return jnp.dot(a, b, preferred_element_type=F32)


def _dot_nt(a, b):
    return lax.dot_general(a, b, (((1,), (1,)), ((), ())), preferred_element_type=F32)


def _dot_tn(a, b):
    return lax.dot_general(a, b, (((0,), (0,)), ((), ())), preferred_element_type=F32)


def _dot_hi(a, b):
    return jnp.dot(a, b, preferred_element_type=F32, precision=lax.Precision.HIGHEST)


def _sigmoid(x):
    return 1.0 / (1.0 + jnp.exp(-x))


def _silu(x):
    return x * _sigmoid(x)


def _t5_bias(dist, t5_ref, heads=NSA_HEADS):
    outs = [jnp.full(dist.shape, t5_ref[0, h], F32) for h in range(heads)]
    for b in range(1, T5_BUCKETS):
        ge = dist >= T5_THR[b - 1]
        outs = [jnp.where(ge, t5_ref[b, h], o) for h, o in enumerate(outs)]
    return outs


def _ada_kernel(c_ref, w_ref, b_ref, o_ref):
    c = _silu(c_ref[...]).astype(BF16)
    o_ref[...] = _dot(c, w_ref[...].astype(BF16)) + b_ref[...]


def ada_call(c, w, b):
    m, d = c.shape
    n = w.shape[1]
    tn = 1152 if n % 1152 == 0 else n
    return pl.pallas_call(
        _ada_kernel,
        grid=(n // tn,),
        in_specs=[pl.BlockSpec((m, d), lambda j: (0, 0)),
                  pl.BlockSpec((d, tn), lambda j: (0, j)),
                  pl.BlockSpec((1, tn), lambda j: (0, j))],
        out_specs=pl.BlockSpec((m, tn), lambda j: (0, j)),
        out_shape=jax.ShapeDtypeStruct((m, n), F32),
        compiler_params=_cparams("arbitrary"),
        name="ada",
    )(c, w, b.reshape(1, n))


def _post_ln(x, y, g, b, alpha):
    z = alpha * x + y
    mu = jnp.mean(z, axis=-1, keepdims=True)
    zc = z - mu
    var = jnp.mean(zc * zc, axis=-1, keepdims=True)
    return zc * lax.rsqrt(var + LN_EPS) * g + b


FF_CHUNK = 256


def _ffn_kernel(x_ref, sh_ref, sc_ref, gt_ref, wg_ref, wu_ref, wd_ref, lg_ref, lb_ref, o_ref, *, alpha):
    x = x_ref[0]
    h = (x * (1.0 + sc_ref[0]) + sh_ref[0]).astype(BF16)
    d_ff = wg_ref.shape[1]
    acc = jnp.zeros(x.shape, F32)
    for c0 in range(0, d_ff, FF_CHUNK):
        g = _dot(h, wg_ref[:, c0:c0 + FF_CHUNK])
        u = _dot(h, wu_ref[:, c0:c0 + FF_CHUNK])
        a = (_silu(g) * u).astype(BF16)
        acc = acc + _dot(a, wd_ref[c0:c0 + FF_CHUNK, :])
    y = 0.5 * (1.0 + gt_ref[0]) * acc
    o_ref[0] = _post_ln(x, y, lg_ref[...], lb_ref[...], alpha)


def ffn_call(x, sh, sc, gt, wg, wu, wd, lg, lb, alpha, tm):
    g_, r_, d = x.shape
    ra = sh.shape[1]
    d_ff = wg.shape[1]
    assert d_ff % FF_CHUNK == 0 and r_ % tm == 0
    if ra == 1:
        a_spec = pl.BlockSpec((1, 1, d), lambda g, i: (g, 0, 0))
    else:
        a_spec = pl.BlockSpec((1, tm, d), lambda g, i: (g, i, 0))
    const = dict(pipeline_mode=pl.Buffered(1))
    return pl.pallas_call(
        functools.partial(_ffn_kernel, alpha=alpha),
        grid=(g_, r_ // tm),
        in_specs=[pl.BlockSpec((1, tm, d), lambda g, i: (g, i, 0)), a_spec, a_spec, a_spec,
                  pl.BlockSpec((d, d_ff), lambda g, i: (0, 0), **const),
                  pl.BlockSpec((d, d_ff), lambda g, i: (0, 0), **const),
                  pl.BlockSpec((d_ff, d), lambda g, i: (0, 0), **const),
                  pl.BlockSpec((1, d), lambda g, i: (0, 0)),
                  pl.BlockSpec((1, d), lambda g, i: (0, 0))],
        out_specs=pl.BlockSpec((1, tm, d), lambda g, i: (g, i, 0)),
        out_shape=jax.ShapeDtypeStruct(x.shape, F32),
        compiler_params=_cparams("parallel", "parallel"),
        name="ffn",
    )(x, sh, sc, gt, wg, wu, wd, lg.reshape(1, d), lb.reshape(1, d))


PROJ = {
    "fox_qp": (0, 512),
    "nsa_q": (512, 256),
    "nsa_rows": (768, 256),
    "hg_q": (1024, 256),
    "hg_f": (1280, 256),
    "hg_i": (1536, 256),
    "hg_g": (1792, 256),
    "fox_kp": (2048, 256),
    "fox_kv": (2304, 256),
    "mla_cq": (2560, 256),
    "win_rows": (2816, 128),
    "ksw": (2944, 128),
    "vsw": (3072, 128),
    "mla_ckv": (3200, 128),
    "misc": (3328, 128),
}
PROJ_W = 3456
MISC_KR, MISC_G, MISC_F = 0, 32, 44


def _proj_src_columns():
    o = dict(zip(("nsa_q", "nsa_kv", "nsa_g", "hg_q", "hg_f", "hg_i", "hg_g", "fox_q", "fox_k", "fox_v",
                  "fox_f", "mla_cq", "mla_ckv", "mla_kr"), IN_OFFS[:-1]))
    src = -np.ones((PROJ_W,), np.int64)

    def put(name, at, cols):
        base = PROJ[name][0] + at
        src[base:base + len(cols)] = cols

    hd = HEAD_DIM
    for h in range(FOX_HEADS):
        put("fox_qp", 128 * h, o["fox_q"] + hd * h + np.arange(hd))
    put("nsa_q", 0, o["nsa_q"] + np.arange(256))
    put("nsa_rows", 0, o["nsa_kv"] + np.arange(256))
    for nm in ("hg_q", "hg_f", "hg_i", "hg_g"):
        put(nm, 0, o[nm] + np.arange(256))
    for j in range(FOX_KV_HEADS):
        put("fox_kp", 128 * j, o["fox_k"] + hd * j + np.arange(hd))
    put("fox_kv", 0, o["fox_k"] + np.arange(256))
    put("mla_cq", 0, o["mla_cq"] + np.arange(MLA_D_CQ))
    put("win_rows", 0, o["nsa_kv"] + 4 * hd + np.arange(2 * hd))
    put("ksw", 0, o["nsa_kv"] + 2 * hd + np.arange(hd))
    put("ksw", hd, o["nsa_kv"] + 4 * hd + np.arange(hd))
    put("vsw", 0, o["nsa_kv"] + 3 * hd + np.arange(hd))
    put("vsw", hd, o["nsa_kv"] + 5 * hd + np.arange(hd))
    put("mla_ckv", 0, o["mla_ckv"] + np.arange(MLA_D_C))
    put("misc", MISC_KR, o["mla_kr"] + np.arange(MLA_D_ROPE))
    put("misc", MISC_G, o["nsa_g"] + np.arange(3 * NSA_HEADS))
    put("misc", MISC_F, o["fox_f"] + np.arange(FOX_HEADS))
    return src


PROJ_SRC = _proj_src_columns()


def pad_w_in(w_in_l):
    cols = jnp.asarray(np.maximum(PROJ_SRC, 0), jnp.int32)
    w = jnp.take(w_in_l, cols, axis=1)
    return jnp.where(jnp.asarray(PROJ_SRC >= 0)[None, :], w, 0.0).astype(BF16)


def pspec(name, tm):
    off, w = PROJ[name]
    return pl.BlockSpec((1, tm, w), lambda g, i, _b=off // w: (g, i, _b))


def _inproj_kernel(x_ref, sh_ref, sc_ref, w_ref, o_ref):
    h = (x_ref[0] * (1.0 + sc_ref[0]) + sh_ref[0]).astype(BF16)
    o_ref[0] = _dot(h, w_ref[...])


def inproj_call(x, sh, sc, w_pad, tm):
    g_, r_, d = x.shape
    ra = sh.shape[1]
    n = w_pad.shape[1]
    if ra == 1:
        a_spec = pl.BlockSpec((1, 1, d), lambda g, i: (g, 0, 0))
    else:
        a_spec = pl.BlockSpec((1, tm, d), lambda g, i: (g, i, 0))
    return pl.pallas_call(
        _inproj_kernel,
        grid=(g_, r_ // tm),
        in_specs=[pl.BlockSpec((1, tm, d), lambda g, i: (g, i, 0)), a_spec, a_spec,
                  pl.BlockSpec((d, n), lambda g, i: (0, 0), pipeline_mode=pl.Buffered(1))],
        out_specs=pl.BlockSpec((1, tm, n), lambda g, i: (g, i, 0)),
        out_shape=jax.ShapeDtypeStruct((g_, r_, n), F32),
        compiler_params=_cparams("parallel", "parallel"),
        name="inproj",
    )(x, sh, sc, w_pad)


MLA_DKP = 256
MLA_SCALE = (MLA_D_NOPE + MLA_D_ROPE) ** -0.5


def mla_weights(qg, kvg, wuq, wuk):
    qg_p = jnp.zeros((1, 256), F32).at[0, :MLA_D_CQ].set(qg)
    dq = MLA_D_NOPE + MLA_D_ROPE
    half = MLA_D_ROPE // 2
    cols = np.zeros((384,), np.int64)
    for h in range(MLA_HEADS):
        cols[64 * h:64 * h + 64] = dq * h + np.arange(64)
        cols[256 + half * h:256 + half * (h + 1)] = dq * h + MLA_D_NOPE + np.arange(half)
        cols[320 + half * h:320 + half * (h + 1)] = dq * h + MLA_D_NOPE + half + np.arange(half)
    wuq_p = jnp.zeros((256, 384), F32).at[:MLA_D_CQ].set(jnp.take(wuq, jnp.asarray(cols, jnp.int32), axis=1))
    wk = wuk.reshape(MLA_D_C, MLA_HEADS, MLA_D_NOPE)
    wcomb = jnp.zeros((384, MLA_HEADS * MLA_DKP), F32)
    perm = np.zeros((384, MLA_HEADS * MLA_DKP), np.float32)
    for h in range(MLA_HEADS):
        wcomb = wcomb.at[64 * h:64 * h + 64, MLA_DKP * h:MLA_DKP * h + MLA_D_C].set(wk[:, h, :].T)
        for i in range(half):
            perm[256 + half * h + i, MLA_DKP * h + MLA_D_C + i] = 1.0
            perm[320 + half * h + i, MLA_DKP * h + MLA_D_C + half + i] = 1.0
    wcomb = wcomb + jnp.asarray(perm)
    return qg_p, kvg.reshape(1, MLA_D_C), wuq_p.astype(BF16), wcomb.astype(BF16)


def rope_tables(pos):
    half = MLA_D_ROPE // 2
    inv = ROPE_BASE ** (-jnp.arange(half, dtype=F32) / half)
    ang = pos.astype(F32)[:, None] * inv[None, :]
    cos, sin = jnp.cos(ang), jnp.sin(ang)
    z = jnp.zeros((pos.shape[0], 128 - 2 * half), F32)
    zh = jnp.zeros_like(cos)
    cos4, sin4 = jnp.tile(cos, (1, MLA_HEADS)), jnp.tile(sin, (1, MLA_HEADS))
    q_c = jnp.concatenate([cos4, cos4], axis=1)
    q_s = jnp.concatenate([-sin4, sin4], axis=1)
    k_c = jnp.concatenate([cos, cos, z], axis=1)
    k_sa = jnp.concatenate([-sin, zh, z], axis=1)
    k_sb = jnp.concatenate([zh, sin, z], axis=1)
    return q_c, q_s, k_c, k_sa, k_sb


def _mla_prep_kernel(cq_ref, ckv_ref, misc_ref, qg_ref, kvg_ref, wuq_ref, wcomb_ref,
                     qc_ref, qs_ref, kc_ref, ksa_ref, ksb_ref, q_ref, rows_ref, kb_ref, vt_ref):
    cq = cq_ref[0]
    ms = jnp.sum(cq * cq, axis=-1, keepdims=True) * (1.0 / MLA_D_CQ)
    qn = ((cq * lax.rsqrt(ms + RMS_EPS)) * qg_ref[...]).astype(BF16)
    qm = _dot(qn, wuq_ref[...])
    t = qm[:, 256:384]
    r = t * qc_ref[...] + pltpu.roll(t, 64, axis=1) * qs_ref[...]
    qcat = jnp.concatenate([qm[:, :256], r], axis=1).astype(BF16)
    q_ref[0] = (_dot(qcat, wcomb_ref[...]) * MLA_SCALE).astype(BF16)
    ckv = ckv_ref[0]
    ms2 = jnp.mean(ckv * ckv, axis=-1, keepdims=True)
    ckvn = (ckv * lax.rsqrt(ms2 + RMS_EPS)) * kvg_ref[...]
    misc = misc_ref[0]
    kr = (misc * kc_ref[...] + pltpu.roll(misc, 128 - MLA_D_ROPE // 2, axis=1) * ksa_ref[...]
          + pltpu.roll(misc, MLA_D_ROPE // 2, axis=1) * ksb_ref[...])
    kfull = jnp.concatenate([ckvn, kr], axis=1)
    rows_ref[0] = kfull[:, :MLA_D_C + MLA_D_ROPE]
    if kb_ref is not None:
        kb_ref[0, 0] = kfull.astype(BF16)
        vt_ref[0, 0] = ckvn.T.astype(BF16)


def mla_prep_call(proj, wts, tabs, tm, with_kv):
    g_, r_, _ = proj.shape
    qg_p, kvg, wuq_p, wcomb = wts
    nt = r_ // tm
    if tabs[0].shape[0] == 1:
        t_spec = pl.BlockSpec((1, 128), lambda g, i: (0, 0))
    else:
        t_spec = pl.BlockSpec((tm, 128), lambda g, i: (i, 0))
    c2 = lambda g, i: (0, 0)
    out_shape = [jax.ShapeDtypeStruct((g_, r_, MLA_HEADS * MLA_DKP), BF16),
                 jax.ShapeDtypeStruct((g_, r_, MLA_D_C + MLA_D_ROPE), F32)]
    out_specs = [pl.BlockSpec((1, tm, MLA_HEADS * MLA_DKP), lambda g, i: (g, i, 0)),
                 pl.BlockSpec((1, tm, MLA_D_C + MLA_D_ROPE), lambda g, i: (g, i, 0))]
    if with_kv:
        out_shape += [jax.ShapeDtypeStruct((g_, nt, tm, MLA_DKP), BF16),
                      jax.ShapeDtypeStruct((g_, nt, MLA_D_C, tm), BF16)]
        out_specs += [pl.BlockSpec((1, 1, tm, MLA_DKP), lambda g, i: (g, i, 0, 0)),
                      pl.BlockSpec((1, 1, MLA_D_C, tm), lambda g, i: (g, i, 0, 0))]
        body = _mla_prep_kernel
    else:
        body = lambda *a: _mla_prep_kernel(*a, None, None)
    return pl.pallas_call(
        body,
        grid=(g_, nt),
        in_specs=[pspec("mla_cq", tm), pspec("mla_ckv", tm), pspec("misc", tm),
                  pl.BlockSpec((1, 256), c2), pl.BlockSpec((1, MLA_D_C), c2),
                  pl.BlockSpec((256, 384), c2), pl.BlockSpec((384, MLA_HEADS * MLA_DKP), c2),
                  t_spec, t_spec, t_spec, t_spec, t_spec],
        out_specs=out_specs,
        out_shape=out_shape,
        compiler_params=_cparams("parallel", "parallel"),
        name="mla_prep",
    )(proj, proj, proj, qg_p, kvg, wuq_p, wcomb, *tabs)


FLASH_GROUP = 4


def _flash_kernel(q_ref, k_ref, vt_ref, o_ref, m_scr, l_scr, acc_scr, *, hpg, dkp, dv, tq, tk):
    qi = pl.program_id(2)
    q = q_ref[0]
    qs = jnp.concatenate([q[:, i * dkp:(i + 1) * dkp] for i in range(hpg)], axis=0)
    r_ = hpg * tq
    m_scr[...] = jnp.full(m_scr.shape, NEG_INF, F32)
    l_scr[...] = jnp.zeros(l_scr.shape, F32)
    acc_scr[...] = jnp.zeros(acc_scr.shape, F32)

    def group(kjs, masked):
        ss = []
        for n_, kj in enumerate(kjs):
            s = _dot_nt(k_ref[0, kj], qs)
            if masked and n_ == len(kjs) - 1:
                kpos = lax.broadcasted_iota(jnp.int32, (tk, r_), 0) + (kj * tk - qi * tq)
                qpos = lax.broadcasted_iota(jnp.int32, (tk, r_), 1) & (tq - 1)
                s = jnp.where(kpos <= qpos, s, NEG_INF)
            ss.append(s)
        m_old = m_scr[...]
        m_new = m_old
        for s in ss:
            m_new = jnp.maximum(m_new, jnp.max(s, axis=0, keepdims=True))
        alpha = jnp.exp(m_old - m_new)
        l_new = alpha * l_scr[...]
        acc = alpha * acc_scr[...]
        for kj, s in zip(kjs, ss):
            p = jnp.exp(s - m_new)
            l_new = l_new + jnp.sum(p, axis=0, keepdims=True)
            acc = acc + _dot(vt_ref[0, kj], p.astype(BF16))
        l_scr[...] = l_new
        acc_scr[...] = acc
        m_scr[...] = m_new

    nfull = (qi * tq) // tk

    ngroups = nfull // FLASH_GROUP

    def body(i, c):
        group([FLASH_GROUP * i + j for j in range(FLASH_GROUP)], False)
        return c

    lax.fori_loop(0, ngroups, body, 0)
    rem = nfull - ngroups * FLASH_GROUP
    for r in range(FLASH_GROUP):
        pl.when(rem == r)(functools.partial(group, [nfull - r + j for j in range(r + 1)], True))

    o = acc_scr[...] / l_scr[...]
    for i in range(hpg):
        o_ref[0, i * dv:(i + 1) * dv, :] = o[:, i * tq:(i + 1) * tq]


def flash_call(q, k_tiles, vt_tiles, *, groups, hpg, dkp, dv, tq):
    n, s, _ = q.shape
    nk, tk = k_tiles.shape[1], k_tiles.shape[2]
    assert tq & (tq - 1) == 0 and tk % tq == 0 and s % tk == 0
    r_ = hpg * tq
    return pl.pallas_call(
        functools.partial(_flash_kernel, hpg=hpg, dkp=dkp, dv=dv, tq=tq, tk=tk),
        grid=(n, groups, s // tq),
        in_specs=[pl.BlockSpec((1, tq, hpg * dkp), lambda b, g, i: (b, i, g)),
                  pl.BlockSpec((1, nk, tk, dkp), lambda b, g, i: (b, 0, 0, g)),
                  pl.BlockSpec((1, nk, dv, tk), lambda b, g, i: (b, 0, g, 0))],
        out_specs=pl.BlockSpec((1, hpg * dv, tq), lambda b, g, i: (b, g, i)),
        out_shape=jax.ShapeDtypeStruct((n, groups * hpg * dv, s), F32),
        scratch_shapes=[pltpu.VMEM((1, r_), F32), pltpu.VMEM((1, r_), F32), pltpu.VMEM((dv, r_), F32)],
        compiler_params=_cparams("parallel", "parallel", "arbitrary"),
        name="flash",
    )(q, k_tiles, vt_tiles)


FOX_AUG0 = HEAD_DIM
FOX_SCALE = HEAD_DIM ** -0.5


def _log_sigmoid(x):
    return -(jnp.maximum(-x, 0.0) + jnp.log1p(jnp.exp(-jnp.abs(x))))


def _split3(c):
    hi = c.astype(BF16)
    r1 = c - hi.astype(F32)
    mid = r1.astype(BF16)
    lo = (r1 - mid.astype(F32)).astype(BF16)
    return hi, mid, lo


def fox_consts():
    g = FOX_HEADS // FOX_KV_HEADS
    place = np.zeros((384, 128 * (FOX_KV_HEADS + FOX_HEADS)), np.float32)
    const = np.zeros((1, 128 * (FOX_KV_HEADS + FOX_HEADS)), np.float32)
    ones0 = FOX_AUG0 + 3 * g
    for h in range(FOX_HEADS):
        j, gi = divmod(h, g)
        for p in range(3):
            place[128 * p + MISC_F + h, 128 * j + FOX_AUG0 + 3 * gi + p] = 1.0
            place[128 * p + MISC_F + h, 128 * (FOX_KV_HEADS + h) + ones0 + p] = 1.0
            const[0, 128 * (FOX_KV_HEADS + h) + FOX_AUG0 + 3 * gi + p] = -1.0
    for j in range(FOX_KV_HEADS):
        const[0, 128 * j + ones0:128 * j + ones0 + 3] = 1.0
    return jnp.asarray(place, BF16), jnp.asarray(const, F32)


def _fox_prep_kernel(qp_ref, kp_ref, kv_ref, misc_ref, bf_ref, tri_ref, place_ref, const_ref,
                     logf_ref, q_ref, k_ref, vt_ref, carry_scr):
    i = pl.program_id(1)

    @pl.when(i == 0)
    def _():
        carry_scr[...] = jnp.zeros(carry_scr.shape, F32)

    logf = _log_sigmoid(misc_ref[0] + bf_ref[...])
    logf_ref[0] = logf[:, MISC_F:MISC_F + FOX_HEADS]
    c = carry_scr[...] + _dot_hi(tri_ref[...], logf)
    tm = c.shape[0]
    carry_scr[...] = c[tm - 1:tm, :]
    hi, mid, lo = _split3(c)
    aug = _dot(jnp.concatenate([hi, mid, lo], axis=1), place_ref[...]) + const_ref[...]
    nk = 128 * FOX_KV_HEADS
    k_ref[0, 0] = (kp_ref[0] + aug[:, :nk]).astype(BF16)
    q_ref[0] = (qp_ref[0] * FOX_SCALE + aug[:, nk:]).astype(BF16)
    vt_ref[0, 0] = kv_ref[0][:, 128:].T.astype(BF16)


def fox_prep_call(proj, bf_row, tri, place, const, tm):
    g_, r_, _ = proj.shape
    nt = r_ // tm
    c2 = lambda g, i: (0, 0)
    return pl.pallas_call(
        _fox_prep_kernel,
        grid=(g_, nt),
        in_specs=[pspec("fox_qp", tm), pspec("fox_kp", tm), pspec("fox_kv", tm), pspec("misc", tm),
                  pl.BlockSpec((1, 128), c2), pl.BlockSpec((tm, tm), c2),
                  pl.BlockSpec(place.shape, c2), pl.BlockSpec(const.shape, c2)],
        out_specs=[pl.BlockSpec((1, tm, FOX_HEADS), lambda g, i: (g, i, 0)),
                   pl.BlockSpec((1, tm, 128 * FOX_HEADS), lambda g, i: (g, i, 0)),
                   pl.BlockSpec((1, 1, tm, 128 * FOX_KV_HEADS), lambda g, i: (g, i, 0, 0)),
                   pl.BlockSpec((1, 1, 128, tm), lambda g, i: (g, i, 0, 0))],
        out_shape=[jax.ShapeDtypeStruct((g_, r_, FOX_HEADS), F32),
                   jax.ShapeDtypeStruct((g_, r_, 128 * FOX_HEADS), BF16),
                   jax.ShapeDtypeStruct((g_, nt, tm, 128 * FOX_KV_HEADS), BF16),
                   jax.ShapeDtypeStruct((g_, nt, 128, tm), BF16)],
        scratch_shapes=[pltpu.VMEM((1, 128), F32)],
        compiler_params=_cparams("parallel", "arbitrary"),
        name="fox_prep",
    )(proj, proj, proj, proj, bf_row, tri, place, const)


def fox_bf_row(fox_bf_l):
    return jnp.zeros((1, 128), F32).at[0, MISC_F:MISC_F + FOX_HEADS].set(fox_bf_l)


HG_SUB = 16
HG_W = HG_HEADS * HG_DK


def hgrn_consts(t):
    r = np.arange(t)
    same = (r[:, None] // HG_SUB) == (r[None, :] // HG_SUB)
    tri = (same & (r[None, :] <= r[:, None])).astype(np.float32)
    tot = same.astype(np.float32)
    c = np.arange(HG_W)
    head = ((c[:, None] // HG_DK) == (c[None, :] // HG_DK)).astype(np.float32)
    return jnp.asarray(tri), jnp.asarray(tot), jnp.asarray(head, BF16), jnp.asarray(head, F32)


def _hgrn_gates(z, lb):
    f = lb + (1.0 - lb) * _sigmoid(z)
    return jnp.log(f), (1.0 - lb) * _sigmoid(-z)


def _hgrn_kernel(q_ref, f_ref, i_ref, lb_ref, tri_ref, tot_ref, eh_ref, mask_ref, o_ref, st_ref,
                 kk_scr, b_scr, v_scr, od_scr, *, t):
    step = pl.program_id(1)

    @pl.when(step == 0)
    def _():
        st_ref[0] = jnp.zeros(st_ref.shape[1:], F32)

    q = q_ref[0]
    v = i_ref[0]
    logf, kk = _hgrn_gates(f_ref[0], lb_ref[...])
    b = _dot_hi(tri_ref[...], logf)
    btot = _dot_hi(tot_ref[...], logf)
    pad = HG_SUB
    zpad = jnp.zeros((pad, HG_W), F32)
    kk_scr[0:pad, :] = zpad
    b_scr[0:pad, :] = zpad
    v_scr[0:pad, :] = zpad
    kk_scr[pad:pad + t, :] = kk
    b_scr[pad:pad + t, :] = b
    v_scr[pad:pad + t, :] = v
    tmod = lax.broadcasted_iota(jnp.int32, (t, HG_W), 0) & (HG_SUB - 1)
    od = jnp.zeros((t, HG_W), F32)
    for dlt in range(HG_SUB):
        ks = kk_scr[pad - dlt:pad - dlt + t, :]
        bs = b_scr[pad - dlt:pad - dlt + t, :]
        vs = v_scr[pad - dlt:pad - dlt + t, :]
        p = jnp.where(tmod >= dlt, q * ks * jnp.exp(jnp.minimum(b - bs, 0.0)), 0.0)
        od = od + _dot(p.astype(BF16), eh_ref[...]) * vs
    od_scr[...] = od
    qt = (q * jnp.exp(b)).astype(BF16)
    kh = (kk * jnp.exp(btot - b)).astype(BF16)
    dec = jnp.exp(btot)
    vb = v.astype(BF16)
    mask = mask_ref[...]
    st = st_ref[0]
    for c in range(t // HG_SUB):
        r0 = c * HG_SUB
        o_ref[0, r0:r0 + HG_SUB, :] = od_scr[r0:r0 + HG_SUB, :] + _dot_nt(qt[r0:r0 + HG_SUB], st.astype(BF16))
        upd = _dot_tn(vb[r0:r0 + HG_SUB], kh[r0:r0 + HG_SUB])
        st = dec[r0:r0 + 1, :] * st + mask * upd
    st_ref[0] = st


def hgrn_call(proj, lb_row, consts, t):
    n, s, _ = proj.shape
    tri, tot, eh, mask = consts
    c2 = lambda g, i: (0, 0)
    return pl.pallas_call(
        functools.partial(_hgrn_kernel, t=t),
        grid=(n, s // t),
        in_specs=[pspec("hg_q", t), pspec("hg_f", t), pspec("hg_i", t),
                  pl.BlockSpec((1, HG_W), c2), pl.BlockSpec((t, t), c2), pl.BlockSpec((t, t), c2),
                  pl.BlockSpec((HG_W, HG_W), c2), pl.BlockSpec((HG_W, HG_W), c2)],
        out_specs=[pl.BlockSpec((1, t, HG_W), lambda g, i: (g, i, 0)),
                   pl.BlockSpec((1, HG_W, HG_W), lambda g, i: (g, 0, 0))],
        out_shape=[jax.ShapeDtypeStruct((n, s, HG_W), F32), jax.ShapeDtypeStruct((n, HG_W, HG_W), F32)],
        scratch_shapes=[pltpu.VMEM((t + HG_SUB, HG_W), F32), pltpu.VMEM((t + HG_SUB, HG_W), F32),
                        pltpu.VMEM((t + HG_SUB, HG_W), F32), pltpu.VMEM((t, HG_W), F32)],
        compiler_params=_cparams("parallel", "arbitrary"),
        name="hgrn",
    )(proj, proj, proj, lb_row, tri, tot, eh, mask)


NSA_TQ = 128
NSA_R = NSA_HEADS * NSA_TQ
NSA_SCALE = HEAD_DIM ** -0.5


def _nsa_pool_kernel(rows_ref, pool_ref, o_ref, ot_ref):
    x = rows_ref[0][:, :2 * HEAD_DIM].astype(BF16)
    pooled = _dot(pool_ref[...], x)
    o_ref[0] = pooled
    ot_ref[0] = pooled.T


def nsa_pool_call(proj, cmp_w_l):
    n, s, _ = proj.shape
    tm = min(s, 4096)
    nb = tm // L_CMP
    w = jax.nn.softmax(cmp_w_l.astype(F32))
    t = np.arange(tm)
    onehot = jnp.asarray((t[None, :] // L_CMP) == np.arange(nb)[:, None])
    pool = jnp.where(onehot, jnp.tile(w, tm // L_CMP)[None, :], 0.0).astype(BF16)
    return pl.pallas_call(
        _nsa_pool_kernel,
        grid=(n, s // tm),
        in_specs=[pspec("nsa_rows", tm), pl.BlockSpec((nb, tm), lambda g, i: (0, 0))],
        out_specs=[pl.BlockSpec((1, nb, 128), lambda g, i: (g, i, 0)),
                   pl.BlockSpec((1, 128, nb), lambda g, i: (g, 0, i))],
        out_shape=[jax.ShapeDtypeStruct((n, s // L_CMP, 128), F32),
                   jax.ShapeDtypeStruct((n, 128, s // L_CMP), F32)],
        compiler_params=_cparams("parallel", "parallel"),
        name="nsa_pool",
    )(proj, pool)


def _nsa_prep_kernel(k_ref, v_ref, ko_ref, vo_ref, *, nt):
    for j in range(nt):
        ko_ref[0, j] = k_ref[0, j * NSA_TQ:(j + 1) * NSA_TQ, :].astype(BF16)
        vo_ref[0, j] = v_ref[0, j * NSA_TQ:(j + 1) * NSA_TQ, :].T.astype(BF16)


def nsa_prep_call(proj, tm):
    n, s, _ = proj.shape
    nt = tm // NSA_TQ
    shp = jax.ShapeDtypeStruct((n, s // NSA_TQ, NSA_TQ, 128), BF16)
    spec = pl.BlockSpec((1, nt, NSA_TQ, 128), lambda g, i: (g, i, 0, 0))
    return pl.pallas_call(
        functools.partial(_nsa_prep_kernel, nt=nt),
        grid=(n, s // tm),
        in_specs=[pspec("ksw", tm), pspec("vsw", tm)],
        out_specs=[spec, spec],
        out_shape=[shp, shp],
        compiler_params=_cparams("parallel", "parallel"),
        name="nsa_prep",
    )(proj, proj)


def _bias_c_kernel(t5_ref, o_ref):
    qi = pl.program_id(0)
    ncb = o_ref.shape[1]
    blk = lax.broadcasted_iota(jnp.int32, (ncb, NSA_TQ), 0)
    r = lax.broadcasted_iota(jnp.int32, (ncb, NSA_TQ), 1)
    dist = qi * NSA_TQ + r - (blk * L_CMP + L_CMP - 1)
    for h, bias in enumerate(_t5_bias(dist, t5_ref)):
        o_ref[0, :, h * NSA_TQ:(h + 1) * NSA_TQ] = bias


def bias_c_call(t5_table, s):
    nq, ncb = s // NSA_TQ, s // L_CMP
    return pl.pallas_call(
        _bias_c_kernel,
        grid=(nq,),
        in_specs=[pl.BlockSpec(memory_space=pltpu.SMEM)],
        out_specs=pl.BlockSpec((1, ncb, NSA_R), lambda i: (i, 0, 0)),
        out_shape=jax.ShapeDtypeStruct((nq, ncb, NSA_R), F32),
        compiler_params=_cparams("parallel"),
        name="bias_c",
    )(t5_table)


def _bias_tile_kernel(t5_ref, o_ref):
    d = pl.program_id(0)
    c = lax.broadcasted_iota(jnp.int32, (NSA_TQ, NSA_TQ), 0)
    r = lax.broadcasted_iota(jnp.int32, (NSA_TQ, NSA_TQ), 1)
    dist = d * NSA_TQ + r - c
    for h, bias in enumerate(_t5_bias(dist, t5_ref)):
        o_ref[0, :, h * NSA_TQ:(h + 1) * NSA_TQ] = bias - t5_ref[T5_BUCKETS - 1, h]


def bias_tile_call(t5_table, nd):
    return pl.pallas_call(
        _bias_tile_kernel,
        grid=(nd,),
        in_specs=[pl.BlockSpec(memory_space=pltpu.SMEM)],
        out_specs=pl.BlockSpec((1, NSA_TQ, NSA_R), lambda i: (i, 0, 0)),
        out_shape=jax.ShapeDtypeStruct((nd, NSA_TQ, NSA_R), F32),
        compiler_params=_cparams("parallel"),
        name="bias_tile",
    )(t5_table)


NSA_WIN_TILES = WINDOW // NSA_TQ + 1
NSA_FAR_GROUP = 8
NSA_NEAR_TILES = 2
assert (NSA_NEAR_TILES - 1) * NSA_TQ + 1 >= T5_THR[-1]


def _online_update_one(tile_fn, d, m_scr, l_scr, acc_scr):
    _online_update([tile_fn(d)], m_scr, l_scr, acc_scr)


def _online_update(tiles, m_scr, l_scr, acc_scr):
    ss = [s if mask is None else jnp.where(mask, s, NEG_INF) for s, mask, _ in tiles]
    m_old = m_scr[...]
    m_new = m_old
    for s in ss:
        m_new = jnp.maximum(m_new, jnp.max(s, axis=0, keepdims=True))
    alpha = jnp.exp(m_old - m_new)
    l_new = alpha * l_scr[...]
    acc = alpha * acc_scr[...]
    for s, (_, _, vt) in zip(ss, tiles):
        p = jnp.exp(s - m_new)
        l_new = l_new + jnp.sum(p, axis=0, keepdims=True)
        acc = acc + _dot(vt, p.astype(BF16))
    l_scr[...] = l_new
    acc_scr[...] = acc
    m_scr[...] = m_new


def _nsa_kernel(q_ref, misc_ref, pooled_ref, pooledt_ref, k_ref, vt_ref, bc_ref, bt_ref,
                o_ref, psum_scr, sel_scr, m_s, l_s, acc_s, m_w, l_w, acc_w, *, k_top):
    qi = pl.program_id(1)
    tq, r_ = NSA_TQ, NSA_R
    hd = HEAD_DIM
    q = q_ref[0] * NSA_SCALE
    z64 = jnp.zeros((tq, hd), F32)
    qh = [q[:, hd * h:hd * (h + 1)] for h in range(NSA_HEADS)]
    qs_sel = jnp.concatenate([jnp.concatenate([x, z64], axis=1) for x in qh], axis=0).astype(BF16)
    qs_win = jnp.concatenate([jnp.concatenate([z64, x], axis=1) for x in qh], axis=0).astype(BF16)
    lane_q = lax.broadcasted_iota(jnp.int32, (1, r_), 1) & (tq - 1)
    pos_row = qi * tq + lane_q

    ncb = pooled_ref.shape[1]
    s_c = _dot_nt(pooled_ref[0].astype(BF16), qs_sel) + bc_ref[0]
    blk_c = lax.broadcasted_iota(jnp.int32, (ncb, r_), 0)
    valid_c = blk_c * L_CMP + (L_CMP - 1) <= pos_row
    s_c = jnp.where(valid_c, s_c, NEG_INF)
    m_c = jnp.max(s_c, axis=0, keepdims=True)
    e_c = jnp.where(valid_c, jnp.exp(s_c - m_c), 0.0)
    p_c = e_c / jnp.maximum(jnp.sum(e_c, axis=0, keepdims=True), TINY)
    o_c = _dot(pooledt_ref[0][hd:2 * hd, :].astype(BF16), p_c.astype(BF16))

    psum = p_c[:, 0:tq]
    for h in range(1, NSA_HEADS):
        psum = psum + p_c[:, h * tq:(h + 1) * tq]
    psum_scr[...] = psum
    nsb = ncb // 2
    imp = psum_scr[pl.ds(0, nsb, stride=2), :] + psum_scr[pl.ds(1, nsb, stride=2), :]
    blk = lax.broadcasted_iota(jnp.int32, (nsb, tq), 0)
    cur = (qi * tq + lax.broadcasted_iota(jnp.int32, (nsb, tq), 1)) // L_SEL
    forced = (blk == 0) | (blk == cur) | (blk == cur - 1)
    score = jnp.where(forced, FORCE_SCORE, jnp.where(blk <= cur, imp, -1.0))
    sel = jnp.zeros((nsb, tq), F32)
    for _ in range(k_top):
        mx = jnp.max(score, axis=0, keepdims=True)
        first = jnp.min(jnp.where(score == mx, blk, nsb), axis=0, keepdims=True)
        pick = blk == first
        sel = jnp.where(pick, 1.0, sel)
        score = jnp.where(pick, -jnp.inf, score)
    sel_scr[...] = sel

    for ref, val in ((m_s, NEG_INF), (l_s, 0.0), (acc_s, 0.0), (m_w, NEG_INF), (l_w, 0.0), (acc_w, 0.0)):
        ref[...] = jnp.full(ref.shape, val, F32)
    row = lax.broadcasted_iota(jnp.int32, (tq, tq), 0)
    krow = lax.broadcasted_iota(jnp.int32, (tq, r_), 0)
    causal = krow <= lane_q

    def sel_mask(kj):
        r0 = sel_scr[pl.ds(2 * kj, 1), :]
        r1 = sel_scr[pl.ds(2 * kj + 1, 1), :]
        m1 = jnp.where(row < L_SEL, r0, r1) > 0.0
        return jnp.concatenate([m1] * NSA_HEADS, axis=1)

    def sel_tile(kj, bias, extra):
        s = _dot_nt(k_ref[0, kj], qs_sel)
        if bias is not None:
            s = s + bias
        mask = sel_mask(kj)
        if extra is not None:
            mask = jnp.logical_and(mask, extra)
        return s, mask, vt_ref[0, kj][0:hd, :]

    nfar = jnp.maximum(qi - 1, 0)

    def far_group(g, c):
        _online_update([sel_tile(NSA_FAR_GROUP * g + j, None, None) for j in range(NSA_FAR_GROUP)], m_s, l_s, acc_s)
        return c

    def far_single(kj, c):
        _online_update([sel_tile(kj, None, None)], m_s, l_s, acc_s)
        return c

    ngroups = nfar // NSA_FAR_GROUP
    lax.fori_loop(0, ngroups, far_group, 0)
    lax.fori_loop(ngroups * NSA_FAR_GROUP, nfar, far_single, 0)

    @pl.when(qi >= 1)
    def _():
        _online_update([sel_tile(qi - 1, bt_ref[1], None), sel_tile(qi, bt_ref[0], causal)], m_s, l_s, acc_s)

    @pl.when(qi == 0)
    def _():
        _online_update([sel_tile(qi, bt_ref[0], causal)], m_s, l_s, acc_s)

    def win_tile(d):
        kj = qi - d
        s = _dot_nt(k_ref[0, kj], qs_win)
        if d < NSA_NEAR_TILES:
            s = s + bt_ref[d]
        if d == 0:
            mask = causal
        elif d == NSA_WIN_TILES - 1:
            mask = krow >= lane_q
        else:
            mask = None
        return s, mask, vt_ref[0, kj][hd:2 * hd, :]

    @pl.when(qi >= NSA_WIN_TILES - 1)
    def _():
        _online_update([win_tile(d) for d in range(NSA_WIN_TILES - 1, -1, -1)], m_w, l_w, acc_w)

    @pl.when(qi < NSA_WIN_TILES - 1)
    def _():
        for d in range(NSA_WIN_TILES - 2, 0, -1):
            pl.when(qi >= d)(functools.partial(_online_update_one, win_tile, d, m_w, l_w, acc_w))
        _online_update([win_tile(0)], m_w, l_w, acc_w)

    gt = _sigmoid(misc_ref[0]).T
    def gate(j):
        return jnp.concatenate([gt[MISC_G + 3 * h + j:MISC_G + 3 * h + j + 1, :] for h in range(NSA_HEADS)], axis=1)
    o_s = acc_s[...] / jnp.maximum(l_s[...], TINY)
    o_w = acc_w[...] / jnp.maximum(l_w[...], TINY)
    o = gate(0) * o_c + gate(1) * o_s + gate(2) * o_w
    for h in range(NSA_HEADS):
        o_ref[0, hd * h:hd * (h + 1), :] = o[:, h * tq:(h + 1) * tq]


def nsa_call(proj, pooled, pooled_t, k_tiles, vt_tiles, bias_c, bias_t):
    n, s, _ = proj.shape
    tq = NSA_TQ
    nq, ncb = s // tq, s // L_CMP
    nsb = ncb // 2
    k_top = min(N_SEL, nsb)
    off, w = PROJ["nsa_q"]
    moff, mw = PROJ["misc"]
    return pl.pallas_call(
        functools.partial(_nsa_kernel, k_top=k_top),
        grid=(n, nq),
        in_specs=[pl.BlockSpec((1, tq, w), lambda g, i: (g, i, off // w)),
                  pl.BlockSpec((1, tq, mw), lambda g, i: (g, i, moff // mw)),
                  pl.BlockSpec((1, ncb, 128), lambda g, i: (g, 0, 0)),
                  pl.BlockSpec((1, 128, ncb), lambda g, i: (g, 0, 0)),
                  pl.BlockSpec((1, nq, tq, 128), lambda g, i: (g, 0, 0, 0)),
                  pl.BlockSpec((1, nq, 128, tq), lambda g, i: (g, 0, 0, 0)),
                  pl.BlockSpec((1, ncb, NSA_R), lambda g, i: (i, 0, 0)),
                  pl.BlockSpec((NSA_NEAR_TILES, tq, NSA_R), lambda g, i: (0, 0, 0))],
        out_specs=pl.BlockSpec((1, NSA_HEADS * HEAD_DIM, tq), lambda g, i: (g, 0, i)),
        out_shape=jax.ShapeDtypeStruct((n, NSA_HEADS * HEAD_DIM, s), F32),
        scratch_shapes=[pltpu.VMEM((ncb, tq), F32), pltpu.VMEM((nsb, tq), F32),
                        pltpu.VMEM((1, NSA_R), F32), pltpu.VMEM((1, NSA_R), F32), pltpu.VMEM((HEAD_DIM, NSA_R), F32),
                        pltpu.VMEM((1, NSA_R), F32), pltpu.VMEM((1, NSA_R), F32), pltpu.VMEM((HEAD_DIM, NSA_R), F32)],
        compiler_params=_cparams("parallel", "arbitrary"),
        name="nsa",
    )(proj, proj, pooled, pooled_t, k_tiles, vt_tiles, bias_c, bias_t)


def _group_rms(x, g):
    return (x * lax.rsqrt(jnp.mean(x * x, axis=-1, keepdims=True) + RMS_EPS)) * g


def _merge_kernel(x_ref, gt_ref, nsa_ref, hg_ref, fox_ref, lat_ref, hgg_ref, mg_ref, wuv_ref, wo_ref,
                  lg_ref, lb_ref, o_ref, *, alpha, tr):
    if tr:
        o_n = nsa_ref[0].T
        o_f = fox_ref[0].T
        o_m = _dot(wuv_ref[...], lat_ref[0].astype(BF16)).T
    else:
        o_n = nsa_ref[0]
        o_f = fox_ref[0]
        o_m = _dot(lat_ref[0].astype(BF16), wuv_ref[...])
    gw = GROUP_W
    mg = mg_ref[...]
    parts = [_group_rms(o_n, mg[:, 0:gw]),
             _group_rms(hg_ref[0], mg[:, gw:2 * gw]) * _silu(hgg_ref[0]),
             _group_rms(o_f, mg[:, 2 * gw:3 * gw]),
             _group_rms(o_m, mg[:, 3 * gw:4 * gw])]
    merged = jnp.concatenate(parts, axis=1).astype(BF16)
    y = (1.0 + gt_ref[0]) * _dot(merged, wo_ref[...])
    o_ref[0] = _post_ln(x_ref[0], y, lg_ref[...], lb_ref[...], alpha)


def merge_call(x, gt, o_nsa, o_hg, o_fox, o_lat, proj, mix_g_l, wuv, w_out, lg, lb, alpha, tm, tr):
    g_, r_, d = x.shape
    ra = gt.shape[1]
    a_spec = (pl.BlockSpec((1, 1, d), lambda g, i: (g, 0, 0)) if ra == 1
              else pl.BlockSpec((1, tm, d), lambda g, i: (g, i, 0)))
    if tr:
        f_spec = lambda w: pl.BlockSpec((1, w, tm), lambda g, i: (g, 0, i))
    else:
        f_spec = lambda w: pl.BlockSpec((1, tm, w), lambda g, i: (g, i, 0))
    c2 = lambda g, i: (0, 0)
    return pl.pallas_call(
        functools.partial(_merge_kernel, alpha=alpha, tr=tr),
        grid=(g_, r_ // tm),
        in_specs=[pl.BlockSpec((1, tm, d), lambda g, i: (g, i, 0)), a_spec,
                  f_spec(GROUP_W), pl.BlockSpec((1, tm, GROUP_W), lambda g, i: (g, i, 0)), f_spec(GROUP_W),
                  f_spec(MLA_HEADS * MLA_D_C), pspec("hg_g", tm),
                  pl.BlockSpec((1, MIX_W), c2), pl.BlockSpec(wuv.shape, c2), pl.BlockSpec((MIX_W, d), c2),
                  pl.BlockSpec((1, d), c2), pl.BlockSpec((1, d), c2)],
        out_specs=pl.BlockSpec((1, tm, d), lambda g, i: (g, i, 0)),
        out_shape=jax.ShapeDtypeStruct(x.shape, F32),
        compiler_params=_cparams("parallel", "parallel"),
        name="merge",
    )(x, gt, o_nsa, o_hg, o_fox, o_lat, proj, mix_g_l.reshape(1, MIX_W), wuv, w_out,
      lg.reshape(1, d), lb.reshape(1, d))


def mla_wuv(wuv_l, tr):
    wv = wuv_l.reshape(MLA_D_C, MLA_HEADS, MLA_D_V)
    bd = jnp.zeros((MLA_HEADS * MLA_D_C, MLA_HEADS * MLA_D_V), F32)
    for h in range(MLA_HEADS):
        bd = bd.at[MLA_D_C * h:MLA_D_C * (h + 1), MLA_D_V * h:MLA_D_V * (h + 1)].set(wv[:, h, :])
    return (bd.T if tr else bd).astype(BF16)


FOX_T = 256
MLA_T = 128
MLA_TK = 256
HG_T = 256


def prompt_mixer(proj, lw, shared):
    n, s, _ = proj.shape
    pooled, pooled_t = nsa_pool_call(proj, lw["cmp_w"])
    k_tiles, vt_tiles = nsa_prep_call(proj, min(s, 512))
    o_nsa = nsa_call(proj, pooled, pooled_t, k_tiles, vt_tiles, shared["bias_c"], shared["bias_t"])
    o_hg, st_t = hgrn_call(proj, lw["lb_row"], shared["hg_consts"], HG_T)
    logf, fq, fk, fvt = fox_prep_call(proj, lw["fox_bf_row"], shared["fox_tri"], shared["fox_place"],
                                      shared["fox_const"], FOX_T)
    o_fox = flash_call(fq, fk, fvt, groups=FOX_KV_HEADS, hpg=FOX_HEADS // FOX_KV_HEADS, dkp=128,
                       dv=HEAD_DIM, tq=FOX_T)
    mq, mrows, mk, mvt = mla_prep_call(proj, lw["mla_w"], shared["rope_p"], MLA_TK, True)
    o_lat = flash_call(mq, mk, mvt, groups=1, hpg=MLA_HEADS, dkp=MLA_DKP, dv=MLA_D_C, tq=MLA_T)
    return (o_nsa, o_hg, o_fox, o_lat), (logf, mrows, st_t)


def layer_weights(l, w_in, w_out, mix_norm_g, nsa_cmp_w, lb_all, fox_b_f, mla_q_norm_g, mla_kv_norm_g,
                  mla_w_uq, mla_w_uk, mla_w_uv, ffn_w_gate, ffn_w_up, ffn_w_down, ln_g, ln_b):
    return dict(
        w_in=pad_w_in(w_in[l]), w_out=w_out[l].astype(BF16), mix_g=mix_norm_g[l], cmp_w=nsa_cmp_w[l],
        lb_row=lb_all[l].reshape(1, HG_W).astype(F32), fox_bf_row=fox_bf_row(fox_b_f[l]),
        mla_w=mla_weights(mla_q_norm_g[l], mla_kv_norm_g[l], mla_w_uq[l], mla_w_uk[l]),
        wuv_t=mla_wuv(mla_w_uv[l], True), wuv=mla_wuv(mla_w_uv[l], False),
        wg=ffn_w_gate[l].astype(BF16), wu=ffn_w_up[l].astype(BF16), wd=ffn_w_down[l].astype(BF16),
        ln_g=ln_g[l], ln_b=ln_b[l])


def prompt_shared(t5_table, s):
    place, const = fox_consts()
    r = np.arange(FOX_T)
    return dict(
        bias_c=bias_c_call(t5_table, s), bias_t=bias_tile_call(t5_table, NSA_NEAR_TILES),
        hg_consts=hgrn_consts(HG_T), fox_tri=jnp.asarray((r[None, :] <= r[:, None]).astype(np.float32)),
        fox_place=place, fox_const=const, rope_p=rope_tables(jnp.arange(s)))


def prompt_layer(x, ada, lw, shared, alpha):
    n, s, d = x.shape
    mod = lambda j, k: ada[:, j, k][:, None, :]
    x = ffn_call(x, mod(0, 0), mod(0, 1), mod(0, 2), lw["wg"][0], lw["wu"][0], lw["wd"][0],
                 lw["ln_g"][0], lw["ln_b"][0], alpha, 512)
    proj = inproj_call(x, mod(1, 0), mod(1, 1), lw["w_in"], 512)
    (o_nsa, o_hg, o_fox, o_lat), (logf, mrows, st_t) = prompt_mixer(proj, lw, shared)
    x = merge_call(x, mod(1, 2), o_nsa, o_hg, o_fox, o_lat, proj, lw["mix_g"], lw["wuv_t"], lw["w_out"],
                   lw["ln_g"][1], lw["ln_b"][1], alpha, 512, True)
    x = ffn_call(x, mod(2, 0), mod(2, 1), mod(2, 2), lw["wg"][1], lw["wu"][1], lw["wd"][1],
                 lw["ln_g"][2], lw["ln_b"][2], alpha, 512)
    o, w = PROJ["nsa_rows"]
    nsa_rows = proj[:, :, o:o + w].reshape(n, s, 4, HEAD_DIM)
    o, w = PROJ["win_rows"]
    new_win = proj[:, s - min(WINDOW, s):, o:o + w].reshape(n, min(WINDOW, s), 2, HEAD_DIM)
    o, w = PROJ["fox_kv"]
    fox_rows = proj[:, :, o:o + w].reshape(n, s, 2, FOX_KV_HEADS, HEAD_DIM)
    return x, (nsa_rows, new_win, fox_rows, logf, mrows, hgrn_state_from_t(st_t))


DEC_PP = 32
QROWS = 8


def _page_specs(pp, block, row_blk, base):
    return [pl.BlockSpec(block, lambda n, j, pt, _i=i: (base + pt[n, j * pp + _i], row_blk, 0)) for i in range(pp)]


def _paged_attn_kernel(pt_ref, q_ref, new_ref, *rest, pp, rk, v0, has_bias):
    if has_bias:
        d_ref, rest = rest[0], rest[1:]
    pages, o_ref, m_scr, l_scr, acc_scr = rest[:pp], rest[pp], rest[pp + 1], rest[pp + 2], rest[pp + 3]
    j = pl.program_id(1)
    dv = acc_scr.shape[1]

    @pl.when(j == 0)
    def _():
        m_scr[...] = jnp.full(m_scr.shape, NEG_INF, F32)
        l_scr[...] = jnp.zeros(l_scr.shape, F32)
        acc_scr[...] = jnp.zeros(acc_scr.shape, F32)

    q = q_ref[0]
    kt = jnp.concatenate([p[0, 0:rk, :].astype(BF16) for p in pages], axis=1)
    vt = jnp.concatenate([p[0, v0:v0 + dv, :].astype(BF16) for p in pages], axis=1)
    s = _dot(q, kt)
    if has_bias:
        s = s + jnp.concatenate([d_ref[0, :, i, :] for i in range(pp)], axis=1)
    m_old = m_scr[...]
    m_new = jnp.maximum(m_old, jnp.max(s, axis=1, keepdims=True))
    alpha = jnp.exp(m_old - m_new)
    p = jnp.exp(s - m_new)
    l_scr[...] = alpha * l_scr[...] + jnp.sum(p, axis=1, keepdims=True)
    acc_scr[...] = alpha * acc_scr[...] + _dot_nt(p.astype(BF16), vt)
    m_scr[...] = m_new

    @pl.when(j == pl.num_programs(1) - 1)
    def _():
        new = new_ref[0].astype(BF16).astype(F32)
        s_new = jnp.sum(q.astype(F32) * new[:, 0:rk], axis=1, keepdims=True)
        m_old = m_scr[...]
        m_f = jnp.maximum(m_old, s_new)
        a = jnp.exp(m_old - m_f)
        p_new = jnp.exp(s_new - m_f)
        l_f = a * l_scr[...] + p_new
        o_ref[0] = (a * acc_scr[...] + p_new.astype(BF16).astype(F32) * new[:, v0:v0 + dv]) / l_f


def paged_attn_call(page_table, pool_t, base, q8, new_rows, bias, rk, v0, dv):
    n, npages = page_table.shape
    r = pool_t.shape[1]
    pp = min(DEC_PP, npages)
    in_specs = [pl.BlockSpec((1, QROWS, rk), lambda b, j, pt: (b, 0, 0)),
                pl.BlockSpec((1, 1, r), lambda b, j, pt: (b, 0, 0))]
    args = [q8, new_rows]
    if bias is not None:
        in_specs.append(pl.BlockSpec((1, QROWS, pp, PAGE_SIZE), lambda b, j, pt: (b, 0, j, 0)))
        args.append(bias)
    in_specs += _page_specs(pp, (1, r, PAGE_SIZE), 0, base)
    args += [pool_t] * pp
    return pl.pallas_call(
        functools.partial(_paged_attn_kernel, pp=pp, rk=rk, v0=v0, has_bias=bias is not None),
        grid_spec=pltpu.PrefetchScalarGridSpec(
            num_scalar_prefetch=1, grid=(n, npages // pp), in_specs=in_specs,
            out_specs=pl.BlockSpec((1, QROWS, dv), lambda b, j, pt: (b, 0, 0)),
            scratch_shapes=[pltpu.VMEM((QROWS, 1), F32), pltpu.VMEM((QROWS, 1), F32), pltpu.VMEM((QROWS, dv), F32)]),
        out_shape=jax.ShapeDtypeStruct((n, QROWS, dv), F32),
        compiler_params=_cparams("parallel", "arbitrary"),
        name="paged_attn",
    )(page_table, *args)


def _fox_dec_bias_kernel(lg_ref, misc_ref, bf_ref, after_ref, later_ref, d_ref, logf_ref):
    logf_new = _log_sigmoid(misc_ref[0] + bf_ref[...])
    logf_ref[0] = logf_new
    ones = jnp.ones((PAGE_SIZE, PAGE_SIZE), F32)
    for h in range(FOX_HEADS):
        x = lg_ref[0, h]
        within = _dot_hi(x, after_ref[...])
        across = _dot_hi(later_ref[...], _dot_hi(x, ones))
        d_ref[0, h] = within + across + logf_new[:, MISC_F + h:MISC_F + h + 1]
    for h in range(FOX_HEADS, QROWS):
        d_ref[0, h] = jnp.zeros(d_ref.shape[2:], F32)


def fox_dec_bias_call(logf_pages, misc_rows, bf_row):
    n, _, npages, _ = logf_pages.shape
    pos = np.arange(PAGE_SIZE)
    after = jnp.asarray((pos[:, None] > pos[None, :]).astype(np.float32))
    pg = np.arange(npages)
    later = jnp.asarray((pg[None, :] > pg[:, None]).astype(np.float32))
    return pl.pallas_call(
        _fox_dec_bias_kernel,
        grid=(n,),
        in_specs=[pl.BlockSpec((1, FOX_HEADS, npages, PAGE_SIZE), lambda b: (b, 0, 0, 0)),
                  pl.BlockSpec((1, 1, 128), lambda b: (b, 0, 0)),
                  pl.BlockSpec((1, 128), lambda b: (0, 0)),
                  pl.BlockSpec((PAGE_SIZE, PAGE_SIZE), lambda b: (0, 0)),
                  pl.BlockSpec((npages, npages), lambda b: (0, 0))],
        out_specs=[pl.BlockSpec((1, QROWS, npages, PAGE_SIZE), lambda b: (b, 0, 0, 0)),
                   pl.BlockSpec((1, 1, 128), lambda b: (b, 0, 0))],
        out_shape=[jax.ShapeDtypeStruct((n, QROWS, npages, PAGE_SIZE), F32),
                   jax.ShapeDtypeStruct((n, 1, 128), F32)],
        compiler_params=_cparams("parallel"),
        name="fox_dec_bias",
    )(logf_pages, misc_rows, bf_row, after, later)


NSA_POOL_PP = 32


def _nsa_pool_dec_kernel(pt_ref, pool_ref, *rest, pp):
    pages, o_ref = rest[:pp], rest[pp]
    x = jnp.concatenate([p[0].astype(BF16) for p in pages], axis=1)
    o_ref[0] = _dot(x, pool_ref[...])


def nsa_pool_dec_call(page_table, pool_t, base, cmp_w_l):
    n, npages = page_table.shape
    pp = min(NSA_POOL_PP, npages)
    tm = pp * PAGE_SIZE
    nb = tm // L_CMP
    w = jax.nn.softmax(cmp_w_l.astype(F32))
    t = np.arange(tm)
    onehot = jnp.asarray((t[:, None] // L_CMP) == np.arange(nb)[None, :])
    pmat = jnp.where(onehot, jnp.tile(w, tm // L_CMP)[:, None], 0.0).astype(BF16)
    return pl.pallas_call(
        functools.partial(_nsa_pool_dec_kernel, pp=pp),
        grid_spec=pltpu.PrefetchScalarGridSpec(
            num_scalar_prefetch=1, grid=(n, npages // pp),
            in_specs=[pl.BlockSpec((tm, nb), lambda b, j, pt: (0, 0))] + _page_specs(pp, (1, 128, PAGE_SIZE), 0, base),
            out_specs=pl.BlockSpec((1, 128, nb), lambda b, j, pt: (b, 0, j))),
        out_shape=jax.ShapeDtypeStruct((n, 128, npages * PAGE_SIZE // L_CMP), F32),
        compiler_params=_cparams("parallel", "parallel"),
        name="nsa_pool_dec",
    )(page_table, pmat, *([pool_t] * pp))


assert T5_THR[-1] <= PAGE_SIZE


def _bias_rows_kernel(t5_ref, oc_ref, ow_ref, od_ref, *, t_len, wb):
    ncb = oc_ref.shape[1]
    dist_c = t_len - (lax.broadcasted_iota(jnp.int32, (1, ncb), 1) * L_CMP + L_CMP - 1)
    dist_w = wb - lax.broadcasted_iota(jnp.int32, (1, wb), 1)
    lane = lax.broadcasted_iota(jnp.int32, (1, 2 * PAGE_SIZE), 1)
    dist_d = jnp.where(lane < PAGE_SIZE, PAGE_SIZE - lane, lane - PAGE_SIZE)
    for ref, dist in ((oc_ref, dist_c), (ow_ref, dist_w), (od_ref, dist_d)):
        ref[...] = jnp.zeros(ref.shape, F32)
        for h, bias in enumerate(_t5_bias(dist, t5_ref)):
            ref[h:h + 1, :] = bias


def bias_rows_call(t5_table, t_len, wb):
    ncb = t_len // L_CMP
    return pl.pallas_call(
        functools.partial(_bias_rows_kernel, t_len=t_len, wb=wb),
        in_specs=[pl.BlockSpec(memory_space=pltpu.SMEM)],
        out_shape=[jax.ShapeDtypeStruct((QROWS, ncb), F32), jax.ShapeDtypeStruct((QROWS, wb), F32),
                   jax.ShapeDtypeStruct((QROWS, 2 * PAGE_SIZE), F32)],
        name="bias_rows",
    )(t5_table)


def _nsa_cmp_dec_kernel(q_ref, pooled_ref, bc_ref, pair_ref, oc_ref, imp_ref):
    kvt = pooled_ref[0].astype(BF16)
    s = _dot(q_ref[0], kvt[0:HEAD_DIM, :]) + bc_ref[...]
    m = jnp.max(s, axis=1, keepdims=True)
    e = jnp.exp(s - m)
    p = e / jnp.maximum(jnp.sum(e, axis=1, keepdims=True), TINY)
    oc_ref[0] = _dot_nt(p.astype(BF16), kvt[HEAD_DIM:2 * HEAD_DIM, :])
    head = lax.broadcasted_iota(jnp.int32, p.shape, 0) < NSA_HEADS
    psum = jnp.sum(jnp.where(head, p, 0.0), axis=0, keepdims=True)
    imp_ref[0] = _dot_hi(jnp.broadcast_to(psum, (QROWS, psum.shape[1])), pair_ref[...])[0:1, :]


def nsa_cmp_dec_call(q8, pooled_t, bias_c):
    n, _, ncb = pooled_t.shape
    b = np.arange(ncb)
    pair = jnp.asarray((b[:, None] // 2 == np.arange(ncb // 2)[None, :]).astype(np.float32))
    return pl.pallas_call(
        _nsa_cmp_dec_kernel,
        grid=(n,),
        in_specs=[pl.BlockSpec((1, QROWS, HEAD_DIM), lambda i: (i, 0, 0)),
                  pl.BlockSpec((1, 128, ncb), lambda i: (i, 0, 0)),
                  pl.BlockSpec((QROWS, ncb), lambda i: (0, 0)),
                  pl.BlockSpec((ncb, ncb // 2), lambda i: (0, 0))],
        out_specs=[pl.BlockSpec((1, QROWS, HEAD_DIM), lambda i: (i, 0, 0)),
                   pl.BlockSpec((1, 1, ncb // 2), lambda i: (i, 0, 0))],
        out_shape=[jax.ShapeDtypeStruct((n, QROWS, HEAD_DIM), F32), jax.ShapeDtypeStruct((n, 1, ncb // 2), F32)],
        compiler_params=_cparams("parallel"),
        name="nsa_cmp_dec",
    )(q8, pooled_t, bias_c, pair)


def _topk_dec_kernel(imp_ref, idx_ref, *, t_len, k_top):
    imp_t = imp_ref[...].T
    nfull, n = imp_t.shape
    rows = idx_ref.shape[0]
    cur = t_len // L_SEL
    score = jnp.concatenate([imp_t, jnp.zeros((SUBLANES, n), F32)], axis=0)
    blk = lax.broadcasted_iota(jnp.int32, score.shape, 0)
    forced = (blk == 0) | (blk == cur) | (blk == cur - 1)
    score = jnp.where(forced, FORCE_SCORE, jnp.where(blk <= cur, score, -jnp.inf))
    big = nfull + SUBLANES
    picks = []
    for _ in range(k_top):
        mx = jnp.max(score, axis=0, keepdims=True)
        first = jnp.min(jnp.where(score == mx, blk, big), axis=0, keepdims=True)
        picks.append(first)
        score = jnp.where(blk == first, -jnp.inf, score)
    picks += [jnp.zeros((1, n), jnp.int32)] * (rows - k_top)
    idx_ref[...] = jnp.concatenate(picks, axis=0)


def topk_dec_call(imp, t_len):
    n = imp.shape[0]
    nsb = -(-(t_len + 1) // L_SEL)
    k_top = min(N_SEL, nsb)
    return pl.pallas_call(
        functools.partial(_topk_dec_kernel, t_len=t_len, k_top=k_top),
        out_shape=jax.ShapeDtypeStruct((N_SEL, n), jnp.int32),
        name="topk_dec",
    )(imp)


def _nsa_selwin_dec_kernel(pt_ref, idx_ref, q_ref, g_ref, oc_ref, snew_ref, wnew_ref, win_ref, bw_ref, bd_ref,
                           *rest, t_len, k_top):
    pages, o_ref = rest[:N_SEL], rest[N_SEL]
    b = pl.program_id(0)
    hd = HEAD_DIM
    q = q_ref[0]
    qf = q.astype(F32)
    cur = t_len // L_SEL
    last_page = t_len // PAGE_SIZE - 1
    bd = bd_ref[...]
    near, far, bias0 = bd[:, 0:PAGE_SIZE], bd[:, 0:1], bd[:, PAGE_SIZE:PAGE_SIZE + 1]
    kt = jnp.concatenate([pg[0, 0:hd, :].astype(BF16) for pg in pages], axis=1)
    vt = jnp.concatenate([pg[0, hd:2 * hd, :].astype(BF16) for pg in pages], axis=1)
    nk = N_SEL * PAGE_SIZE
    lane = lax.broadcasted_iota(jnp.int32, (1, nk), 1)
    slot = lane // PAGE_SIZE
    in_page = lane & (PAGE_SIZE - 1)
    blk = jnp.full((1, nk), -1, jnp.int32)
    biases = []
    far_tile = jnp.broadcast_to(far, (QROWS, PAGE_SIZE))
    for i in range(N_SEL):
        if i < k_top:
            bi = idx_ref[b * N_SEL + i]
            blk = jnp.where(slot == i, bi, blk)
            biases.append(jnp.where(bi // 2 == last_page, near, far_tile))
        else:
            biases.append(far_tile)
    bias = jnp.concatenate(biases, axis=1)
    valid = (blk >= 0) & (blk < cur) & ((in_page // L_SEL) == (blk & 1))
    s = jnp.where(valid, _dot(q, kt) + bias, NEG_INF)
    snew = snew_ref[0].astype(BF16).astype(F32)
    s_new = jnp.sum(qf * snew[:, 0:hd], axis=1, keepdims=True) + bias0
    m = jnp.maximum(jnp.max(s, axis=1, keepdims=True), s_new)
    e = jnp.where(valid, jnp.exp(s - m), 0.0)
    e_new = jnp.exp(s_new - m)
    den = jnp.maximum(jnp.sum(e, axis=1, keepdims=True) + e_new, TINY)
    o_s = (_dot_nt(e.astype(BF16), vt) + e_new.astype(BF16).astype(F32) * snew[:, hd:2 * hd]) / den
    winb = win_ref[0].astype(BF16)
    wnb = wnew_ref[0].astype(BF16).astype(F32)
    s_w = _dot(q, winb[0:hd, :]) + bw_ref[...]
    sw_new = jnp.sum(qf * wnb[:, 0:hd], axis=1, keepdims=True) + bias0
    m_w = jnp.maximum(jnp.max(s_w, axis=1, keepdims=True), sw_new)
    e_w = jnp.exp(s_w - m_w)
    ew_new = jnp.exp(sw_new - m_w)
    den_w = jnp.maximum(jnp.sum(e_w, axis=1, keepdims=True) + ew_new, TINY)
    o_w = (_dot_nt(e_w.astype(BF16), winb[hd:2 * hd, :]) + ew_new.astype(BF16).astype(F32) * wnb[:, hd:2 * hd]) / den_w
    g = _sigmoid(g_ref[0])
    o_ref[0] = g[:, 0:1] * oc_ref[0] + g[:, 1:2] * o_s + g[:, 2:3] * o_w


def nsa_selwin_dec_call(page_table, idx, pool_t, base, q8, g8, o_c, sel_new, win_new, win_t, wbase, bias_w, bias_d,
                        t_len):
    n, npages = page_table.shape
    wb = win_t.shape[2]
    nsb = -(-(t_len + 1) // L_SEL)
    k_top = min(N_SEL, nsb)
    row = lambda w: pl.BlockSpec((1, 1, w), lambda b, pt, ix: (b, 0, 0))
    q_spec = pl.BlockSpec((1, QROWS, HEAD_DIM), lambda b, pt, ix: (b, 0, 0))
    pg_specs = [pl.BlockSpec((1, 128, PAGE_SIZE),
                             lambda b, pt, ix, _i=i: (base + pt[b, jnp.clip(ix[b * N_SEL + _i] // 2, 0, npages - 1)], 1, 0))
                for i in range(N_SEL)]
    return pl.pallas_call(
        functools.partial(_nsa_selwin_dec_kernel, t_len=t_len, k_top=k_top),
        grid_spec=pltpu.PrefetchScalarGridSpec(
            num_scalar_prefetch=2, grid=(n,),
            in_specs=[q_spec, pl.BlockSpec((1, QROWS, 128), lambda b, pt, ix: (b, 0, 0)), q_spec, row(128), row(128),
                      pl.BlockSpec((1, 128, wb), lambda b, pt, ix: (wbase + b, 0, 0)),
                      pl.BlockSpec((QROWS, wb), lambda b, pt, ix: (0, 0)),
                      pl.BlockSpec((QROWS, 2 * PAGE_SIZE), lambda b, pt, ix: (0, 0))] + pg_specs,
            out_specs=q_spec),
        out_shape=jax.ShapeDtypeStruct((n, QROWS, HEAD_DIM), F32),
        compiler_params=_cparams("parallel"),
        name="nsa_selwin_dec",
    )(page_table, idx, q8, g8, o_c, sel_new, win_new, win_t, bias_w, bias_d, *([pool_t] * N_SEL))


def _hgrn_dec_kernel(q_ref, z_ref, v_ref, lb_ref, s_ref, o_ref, so_ref):
    logf, kk = _hgrn_gates(z_ref[0], lb_ref[0])
    f = jnp.exp(logf)
    q = q_ref[0]
    v = v_ref[0]
    acc = jnp.zeros(v.shape, F32)
    for k in range(HG_DK):
        s1 = f[k:k + 1, :] * s_ref[0, k] + kk[k:k + 1, :] * v
        so_ref[0, k] = s1
        acc = acc + q[k:k + 1, :] * s1
    o_ref[0] = acc


def hgrn_dec_call(q_t, z_t, v_t, lb_t, state_t, hbase):
    h = q_t.shape[0]
    _, dk, dv, n = state_t.shape
    r3 = lambda a: pl.BlockSpec((1, a, n), lambda i: (i, 0, 0))
    s_spec = pl.BlockSpec((1, dk, dv, n), lambda i: (i, 0, 0, 0))
    return pl.pallas_call(
        _hgrn_dec_kernel,
        grid=(h,),
        in_specs=[r3(dk), r3(dk), r3(dv), r3(dk), pl.BlockSpec((1, dk, dv, n), lambda i: (hbase + i, 0, 0, 0))],
        out_specs=[r3(dv), s_spec],
        out_shape=[jax.ShapeDtypeStruct((h, dv, n), F32), jax.ShapeDtypeStruct((h, dk, dv, n), F32)],
        compiler_params=_cparams("parallel"),
        name="hgrn_dec",
    )(q_t, z_t, v_t, lb_t, state_t)


def _pad_rows(x, rows=QROWS):
    return jnp.pad(x, ((0, 0), (0, rows - x.shape[1]), (0, 0)))


def decode_caches(cache_nsa_kv, cache_nsa_win, cache_fox_kv, cache_fox_logf, cache_mla, state_hgrn):
    depth, n_phys = cache_nsa_kv.shape[:2]
    n, wb = cache_nsa_win.shape[1:3]
    return dict(
        n_phys=n_phys,
        nsa_t=jnp.transpose(cache_nsa_kv, (0, 1, 3, 4, 2)).reshape(depth * n_phys, 4 * HEAD_DIM, PAGE_SIZE),
        win_t=jnp.transpose(cache_nsa_win, (0, 1, 3, 4, 2)).reshape(depth * n, 2 * HEAD_DIM, wb),
        fox_t=jnp.transpose(cache_fox_kv, (0, 1, 3, 4, 5, 2)).reshape(depth * n_phys, 4 * HEAD_DIM, PAGE_SIZE),
        logf_t=jnp.swapaxes(cache_fox_logf, 2, 3).reshape(depth * n_phys, FOX_HEADS, PAGE_SIZE),
        mla_t=jnp.swapaxes(cache_mla, 2, 3).reshape(depth * n_phys, MLA_D_C + MLA_D_ROPE, PAGE_SIZE),
        hg_t=jnp.transpose(state_hgrn.astype(F32), (0, 2, 3, 4, 1)).reshape(depth * HG_HEADS, HG_DK, HG_DV, n),
        win=cache_nsa_win)


def decode_layer(x, ada, lw, shared, caches, l, page_table, alpha):
    _, n, d = x.shape
    n_phys = caches["n_phys"]
    base = l * n_phys
    npages = page_table.shape[1]
    t_len = npages * PAGE_SIZE
    mod = lambda j, k: ada[:, j, k][None]
    x = ffn_call(x, mod(0, 0), mod(0, 1), mod(0, 2), lw["wg"][0], lw["wu"][0], lw["wd"][0],
                 lw["ln_g"][0], lw["ln_b"][0], alpha, n)
    proj = inproj_call(x, mod(1, 0), mod(1, 1), lw["w_in"], n)
    p2 = proj[0]
    col = lambda name: p2[:, PROJ[name][0]:PROJ[name][0] + PROJ[name][1]]
    misc = col("misc")
    mq, mrows = mla_prep_call(proj, lw["mla_w"], shared["rope_d"], n, False)
    dk_m = MLA_D_C + MLA_D_ROPE
    q8 = _pad_rows(mq[0].reshape(n, MLA_HEADS, MLA_DKP)[:, :, :dk_m])
    o_lat = paged_attn_call(page_table, caches["mla_t"], base, q8, mrows[0][:, None, :], None, dk_m, 0, MLA_D_C)
    o_lat = o_lat[:, :MLA_HEADS].reshape(1, n, MLA_HEADS * MLA_D_C)
    g = FOX_HEADS // FOX_KV_HEADS
    fq = col("fox_qp").reshape(n, FOX_HEADS, 128)[:, :, :HEAD_DIM] * FOX_SCALE
    q_bd = jnp.zeros((n, FOX_HEADS, 2 * HEAD_DIM), F32)
    for h in range(FOX_HEADS):
        q_bd = q_bd.at[:, h, HEAD_DIM * (h // g):HEAD_DIM * (h // g + 1)].set(fq[:, h])
    logf_pages = jnp.swapaxes(caches["logf_t"][page_table + base], 1, 2)
    fbias, logf_row = fox_dec_bias_call(logf_pages, misc[:, None, :], lw["fox_bf_row"])
    o_fox = paged_attn_call(page_table, caches["fox_t"], base, _pad_rows(q_bd).astype(BF16),
                            col("fox_kv")[:, None, :], fbias, 2 * HEAD_DIM, 2 * HEAD_DIM, 2 * HEAD_DIM)
    o_fox = jnp.concatenate([o_fox[:, h, HEAD_DIM * (h // g):HEAD_DIM * (h // g + 1)]
                             for h in range(FOX_HEADS)], axis=-1)[None]
    nsa_t = caches["nsa_t"]
    pooled_t = nsa_pool_dec_call(page_table, nsa_t, base, lw["cmp_w"])
    qn8 = _pad_rows(col("nsa_q").reshape(n, NSA_HEADS, HEAD_DIM) * NSA_SCALE).astype(BF16)
    o_c, imp = nsa_cmp_dec_call(qn8, pooled_t, shared["bias_c_d"])
    idx = topk_dec_call(imp[:, 0, :], t_len).T.reshape(-1)
    g8 = _pad_rows(jnp.pad(misc[:, MISC_G:MISC_G + 3 * NSA_HEADS].reshape(n, NSA_HEADS, 3),
                           ((0, 0), (0, 0), (0, 125))))
    win_rows = col("win_rows")
    o_nsa = nsa_selwin_dec_call(page_table, idx, nsa_t, base, qn8, g8, o_c, col("nsa_rows")[:, None, 2 * HEAD_DIM:],
                                win_rows[:, None, :], caches["win_t"], l * n, shared["bias_w_d"], shared["bias_d_d"],
                                t_len)
    o_nsa = o_nsa[:, :NSA_HEADS].reshape(1, n, GROUP_W)
    win_out = jnp.concatenate([caches["win"][l, :, 1:], win_rows.reshape(n, 1, 2, HEAD_DIM)], axis=1)
    tr = lambda name: col(name).T.reshape(HG_HEADS, HG_DK, n)
    lb_t = jnp.broadcast_to(lw["lb_row"].reshape(HG_HEADS, HG_DK, 1), (HG_HEADS, HG_DK, n))
    o_hg, hg_new = hgrn_dec_call(tr("hg_q"), tr("hg_f"), tr("hg_i"), lb_t, caches["hg_t"], l * HG_HEADS)
    o_hg = o_hg.reshape(GROUP_W, n).T[None]
    x = merge_call(x, mod(1, 2), o_nsa, o_hg, o_fox, o_lat, proj, lw["mix_g"], lw["wuv"], lw["w_out"],
                   lw["ln_g"][1], lw["ln_b"][1], alpha, n, False)
    x = ffn_call(x, mod(2, 0), mod(2, 1), mod(2, 2), lw["wg"][1], lw["wu"][1], lw["wd"][1],
                 lw["ln_g"][2], lw["ln_b"][2], alpha, n)
    st = (col("nsa_rows").reshape(n, 1, 4, HEAD_DIM), win_out,
          col("fox_kv").reshape(n, 1, 2, FOX_KV_HEADS, HEAD_DIM),
          logf_row[:, :, MISC_F:MISC_F + FOX_HEADS], mrows[0][:, None, :],
          jnp.transpose(hg_new, (3, 0, 1, 2)))
    return x, st


def kernel(x_prompt, x_sample, c_prompt, c_sample, page_table, cache_nsa_kv, cache_nsa_win, cache_fox_kv,
           cache_fox_logf, cache_mla, state_hgrn, w_in, w_out, mix_norm_g, nsa_cmp_w, t5_table, hgrn_lb_logits,
           fox_b_f, mla_q_norm_g, mla_kv_norm_g, mla_w_uq, mla_w_uk, mla_w_uv, ffn_w_gate, ffn_w_up, ffn_w_down,
           ada_w, ada_b, ln_g, ln_b):
    depth = w_in.shape[0]
    alpha = (2 * depth) ** 0.25
    nb, s, d = x_prompt.shape
    nd = x_sample.shape[0]
    assert x_sample.shape[1] == 1
    t_len = page_table.shape[1] * PAGE_SIZE
    lb_p = jax.nn.softmax(hgrn_lb_logits.astype(F32), axis=0)
    lb_all = jnp.cumsum(lb_p, axis=0) - lb_p
    shared = prompt_shared(t5_table, s)
    bias_c_d, bias_w_d, bias_d_d = bias_rows_call(t5_table, t_len, cache_nsa_win.shape[2])
    shared.update(bias_c_d=bias_c_d, bias_w_d=bias_w_d, bias_d_d=bias_d_d,
                  rope_d=rope_tables(jnp.full((1,), t_len, jnp.int32)))
    c_all = jnp.concatenate([c_prompt, c_sample], axis=0)
    c_all = jnp.pad(c_all, ((0, -c_all.shape[0] % SUBLANES), (0, 0)))
    xp, xs = x_prompt, x_sample.reshape(1, nd, d)
    caches = decode_caches(cache_nsa_kv, cache_nsa_win, cache_fox_kv, cache_fox_logf, cache_mla, state_hgrn)
    st_p, st_s = [], []
    for l in range(depth):
        lw = layer_weights(l, w_in, w_out, mix_norm_g, nsa_cmp_w, lb_all, fox_b_f, mla_q_norm_g, mla_kv_norm_g,
                           mla_w_uq, mla_w_uk, mla_w_uv, ffn_w_gate, ffn_w_up, ffn_w_down, ln_g, ln_b)
        ada = ada_call(c_all, ada_w[l], ada_b[l]).reshape(-1, N_SUB, 3, d)
        xp, sp = prompt_layer(xp, ada[:nb], lw, shared, alpha)
        xs, ss = decode_layer(xs, ada[nb:nb + nd], lw, shared, caches, l, page_table, alpha)
        st_p.append(sp)
        st_s.append(ss)
    outs = [xp, xs.reshape(nd, 1, d)]
    for group in (st_p, st_s):
        outs += [jnp.stack([t[i] for t in group]) for i in range(6)]
    return tuple(outs)


def hgrn_state_from_t(st_t):
    n = st_t.shape[0]
    blk = st_t.reshape(n, HG_HEADS, HG_DV, HG_HEADS, HG_DK)
    diag = jnp.stack([blk[:, h, :, h, :] for h in range(HG_HEADS)], axis=1)
    return jnp.swapaxes(diag, -1, -2)
```
